```python
import jax, jax.numpy as jnp
from jax import lax
import numpy as np

D_MODEL = 1024
BATCH = 8
SEQ = 2048
DEPTH = 4

SB_HEADS = 8
SB_HEAD_DIM = 64
SB_WIDTH = SB_HEADS * SB_HEAD_DIM
MLA_HEADS = 8
MLA_NOPE_DIM = 64
MLA_ROPE_DIM = 32
MLA_V_DIM = 64
MLA_QK_DIM = MLA_NOPE_DIM + MLA_ROPE_DIM
MLA_Q_RANK = 384
MLA_KV_RANK = 256
MLA_WIDTH = MLA_HEADS * MLA_V_DIM
D_FF = 4 * D_MODEL
BLOCK_Q = 128
ROPE_THETA = 10000.0
NORM_EPS = 1e-6
N_MOD = 6
IN_WIDTHS = (SB_WIDTH, SB_WIDTH, SB_WIDTH, MLA_Q_RANK, MLA_KV_RANK, MLA_ROPE_DIM, D_MODEL, D_MODEL)
IN_DIM = sum(IN_WIDTHS)

kernel_name = "hybrid_stickbreaking_mla_sqrelu_adaln"


def _rms_norm(x, g):
    xf = x.astype(jnp.float32)
    y = xf * lax.rsqrt(jnp.mean(xf * xf, axis=-1, keepdims=True) + NORM_EPS)
    return y.astype(x.dtype) * g


def _split_cols(p, widths):
    outs, start = [], 0
    for w in widths:
        outs.append(p[..., start:start + w])
        start += w
    return outs


def _rope_tables(positions):
    inv_freq = 1.0 / (ROPE_THETA ** (jnp.arange(0, MLA_ROPE_DIM, 2, dtype=jnp.float32) / MLA_ROPE_DIM))
    ang = positions.astype(jnp.float32)[..., None] * inv_freq
    return jnp.cos(ang), jnp.sin(ang)


def _apply_rope(t, cos, sin):
    half = t.shape[-1] // 2
    t1, t2 = t[..., :half], t[..., half:]
    cs = cos[:, :, None, :].astype(t.dtype)
    sn = sin[:, :, None, :].astype(t.dtype)
    return jnp.concatenate([t1 * cs - t2 * sn, t2 * cs + t1 * sn], axis=-1)


def _stick_breaking_weights(z, mask):
    log_fail = jnp.where(mask, jax.nn.log_sigmoid(-z), 0.0)
    later = lax.cumsum(log_fail, axis=3, reverse=True) - log_fail
    return jnp.where(mask, jnp.exp(jax.nn.log_sigmoid(z) + later), 0.0)


def _softmax_weights(z, mask):
    return jax.nn.softmax(jnp.where(mask, z, -jnp.inf), axis=-1)


def _causal_block_attention(q, k, v, weight_fn, strict):
    seq = q.shape[1]
    scale = q.shape[-1] ** -0.5
    outs = []
    for t0 in range(0, seq, BLOCK_Q):
        end = t0 + BLOCK_Q
        z = jnp.einsum('bqhd,bkhd->bhqk', q[:, t0:end].astype(jnp.float32),
                       k[:, :end].astype(jnp.float32)) * scale
        t_idx = t0 + jnp.arange(BLOCK_Q)[:, None]
        s_idx = jnp.arange(end)[None, :]
        mask = (s_idx < t_idx) if strict else (s_idx <= t_idx)
        w = weight_fn(z, mask)
        outs.append(jnp.einsum('bhqk,bkhd->bqhd', w.astype(v.dtype), v[:, :end]))
    return jnp.concatenate(outs, axis=1)


def setup_inputs(seed: int = 0) -> dict:
    key = jax.random.key(seed)
    ks = jax.random.split(key, 20)

    def nrm(k, shape, fan_in):
        return jax.random.normal(k, shape, jnp.float32) * (fan_in ** -0.5)

    def gain(k, shape):
        return 1.0 + 0.05 * jax.random.normal(k, shape, jnp.float32)

    x = jax.random.normal(ks[0], (BATCH, SEQ, D_MODEL), jnp.float32)
    c = jax.random.normal(ks[1], (BATCH, D_MODEL), jnp.float32)
    offsets = jax.random.randint(ks[2], (BATCH, 1), 0, 1024, dtype=jnp.int32)
    positions = (offsets + jnp.arange(SEQ, dtype=jnp.int32)[None, :]).astype(jnp.int32)
    return {
        "x": x,
        "c": c,
        "positions": positions,
        "w_ada": nrm(ks[3], (DEPTH, D_MODEL, N_MOD * D_MODEL), D_MODEL),
        "b_ada": 0.02 * jax.random.normal(ks[4], (DEPTH, N_MOD * D_MODEL), jnp.float32),
        "g_mix_norm": gain(ks[5], (DEPTH, D_MODEL)),
        "w_in": nrm(ks[6], (DEPTH, D_MODEL, IN_DIM), D_MODEL),
        "g_q_lat": gain(ks[7], (DEPTH, MLA_Q_RANK)),
        "w_q_up": nrm(ks[8], (DEPTH, MLA_Q_RANK, MLA_HEADS * MLA_QK_DIM), MLA_Q_RANK),
        "g_kv_lat": gain(ks[9], (DEPTH, MLA_KV_RANK)),
        "w_kv_up": nrm(ks[10], (DEPTH, MLA_KV_RANK, MLA_HEADS * (MLA_NOPE_DIM + MLA_V_DIM)), MLA_KV_RANK),
        "w_sb_out": nrm(ks[11], (DEPTH, SB_WIDTH, D_MODEL), SB_WIDTH),
        "w_mla_out": nrm(ks[12], (DEPTH, MLA_WIDTH, D_MODEL), MLA_WIDTH),
        "w_mix_out": nrm(ks[13], (DEPTH, D_MODEL, D_MODEL), D_MODEL),
        "g_mlp_norm": gain(ks[14], (DEPTH, D_MODEL)),
        "w_up": nrm(ks[15], (DEPTH, D_MODEL, D_FF), D_MODEL),
        "w_down": nrm(ks[16], (DEPTH, D_FF, D_MODEL), D_FF),
        "g_final": gain(ks[17], (D_MODEL,)),
    }


def reference(x, c, positions, w_ada, b_ada, g_mix_norm, w_in, g_q_lat, w_q_up,
              g_kv_lat, w_kv_up, w_sb_out, w_mla_out, w_mix_out, g_mlp_norm,
              w_up, w_down, g_final):
    B, S, _ = x.shape
    cos, sin = _rope_tables(positions)
    c_act = jax.nn.silu(c)
    for l in range(DEPTH):
        mod = (c_act @ w_ada[l] + b_ada[l])[:, None, :]
        shift1, scale1, gate1, shift2, scale2, gate2 = jnp.split(mod, N_MOD, axis=-1)

        h = _rms_norm(x, g_mix_norm[l]) * (1.0 + scale1) + shift1
        p = h @ w_in[l]
        q_sb, k_sb, v_sb, q_lat, kv_lat, k_rope, gate_sb, gate_mla = _split_cols(p, IN_WIDTHS)

        o_sb = _causal_block_attention(
            q_sb.reshape(B, S, SB_HEADS, SB_HEAD_DIM),
            k_sb.reshape(B, S, SB_HEADS, SB_HEAD_DIM),
            v_sb.reshape(B, S, SB_HEADS, SB_HEAD_DIM),
            _stick_breaking_weights, strict=True)
        o_sb = o_sb.reshape(B, S, SB_WIDTH) @ w_sb_out[l]

        q = (_rms_norm(q_lat, g_q_lat[l]) @ w_q_up[l]).reshape(B, S, MLA_HEADS, MLA_QK_DIM)
        kv = (_rms_norm(kv_lat, g_kv_lat[l]) @ w_kv_up[l]).reshape(
            B, S, MLA_HEADS, MLA_NOPE_DIM + MLA_V_DIM)
        k_nope, v_mla = kv[..., :MLA_NOPE_DIM], kv[..., MLA_NOPE_DIM:]
        q_full = jnp.concatenate(
            [q[..., :MLA_NOPE_DIM], _apply_rope(q[..., MLA_NOPE_DIM:], cos, sin)], axis=-1)
        k_pe = _apply_rope(k_rope[:, :, None, :], cos, sin)
        k_full = jnp.concatenate(
            [k_nope, jnp.broadcast_to(k_pe, (B, S, MLA_HEADS, MLA_ROPE_DIM))], axis=-1)
        o_mla = _causal_block_attention(q_full, k_full, v_mla, _softmax_weights, strict=False)
        o_mla = o_mla.reshape(B, S, MLA_WIDTH) @ w_mla_out[l]

        merged = jax.nn.sigmoid(gate_sb) * o_sb + jax.nn.sigmoid(gate_mla) * o_mla
        x = x + gate1 * (merged @ w_mix_out[l])

        h = _rms_norm(x, g_mlp_norm[l]) * (1.0 + scale2) + shift2
        x = x + gate2 * (jnp.square(jax.nn.relu(h @ w_up[l])) @ w_down[l])

    return _rms_norm(x, g_final)
```

```python
import functools

import jax
import jax.numpy as jnp
from jax import lax
from jax.experimental import pallas as pl
from jax.experimental.pallas import tpu as pltpu

D_MODEL = 1024
SB_HEADS = 8
HEAD_DIM = 64
SB_WIDTH = SB_HEADS * HEAD_DIM
MLA_HEADS = 8
MLA_ROPE_DIM = 32
MLA_QK_DIM = HEAD_DIM + MLA_ROPE_DIM
MLA_Q_RANK = 384
MLA_KV_RANK = 256
MLA_WIDTH = MLA_HEADS * HEAD_DIM
D_FF = 4 * D_MODEL
ROPE_THETA = 10000.0
NORM_EPS = 1e-6
N_MOD = 6

LANES = 128
PAIR_W = 2 * LANES
N_PAIRS = SB_HEADS // 2
MLA_QK_W = N_PAIRS * PAIR_W

OFF_QKV = 0
OFF_QLAT = 3 * SB_WIDTH
OFF_KVLAT = OFF_QLAT + MLA_Q_RANK
OFF_KROPE = OFF_KVLAT + MLA_KV_RANK
OFF_GATES = OFF_KROPE + LANES
IN_W = OFF_GATES + 2 * D_MODEL

ROW_TILE = 256
ATTN_TILE = 256
VMEM_LIMIT = 56 * 1024 * 1024

F32 = jnp.float32
BF16 = jnp.bfloat16


def _resident(shape):
    nd = len(shape)
    return pl.BlockSpec(shape, lambda *_: (0,) * nd, pipeline_mode=pl.Buffered(1))


def _rms(x, g):
    return x * lax.rsqrt(jnp.mean(x * x, axis=-1, keepdims=True) + NORM_EPS) * g


def _dot(a, b):
    return jnp.dot(a, b, preferred_element_type=F32)


def _dot_nt(a, b):
    return lax.dot_general(a, b, (((1,), (1,)), ((), ())), preferred_element_type=F32)


def _mod_kernel(c_ref, w_ref, b_ref, o_ref):
    c = c_ref[...]
    c_act = c * (1.0 / (1.0 + jnp.exp(-c)))
    o_ref[0] = _dot(c_act.astype(BF16), w_ref[0].astype(BF16)) + b_ref[0]


def _mod_call(c, w_ada, b_ada):
    depth, d, n = w_ada.shape
    batch = c.shape[0]
    tn = 1536
    return pl.pallas_call(
        _mod_kernel,
        out_shape=jax.ShapeDtypeStruct((depth, batch, n), F32),
        grid=(depth, n // tn),
        in_specs=[
            pl.BlockSpec((batch, d), lambda l, j: (0, 0)),
            pl.BlockSpec((1, d, tn), lambda l, j: (l, 0, j)),
            pl.BlockSpec((1, 1, tn), lambda l, j: (l, 0, j)),
        ],
        out_specs=pl.BlockSpec((1, batch, tn), lambda l, j: (l, 0, j)),
        compiler_params=pltpu.CompilerParams(
            dimension_semantics=("parallel", "parallel"), vmem_limit_bytes=VMEM_LIMIT),
        name="adaln_mod",
    )(c, w_ada, b_ada.reshape(depth, 1, n))


def _rope_table_kernel(pos_ref, inv_ref, sign_ref, cos_ref, sin_ref):
    ang = pos_ref[...].astype(F32) * inv_ref[...]
    cos_ref[...] = jnp.cos(ang)
    sin_ref[...] = jnp.sin(ang) * sign_ref[...]


def _rope_tables(positions):
    t = positions.size
    half = MLA_ROPE_DIM // 2
    inv_freq = 1.0 / (ROPE_THETA ** (jnp.arange(0, MLA_ROPE_DIM, 2, dtype=F32) / MLA_ROPE_DIM))
    lane = jnp.arange(LANES)
    inv_lane = inv_freq[lane % half].reshape(1, LANES)
    sign_lane = jnp.where((lane % MLA_ROPE_DIM) < half, -1.0, 1.0).astype(F32).reshape(1, LANES)
    ts = 2048
    return pl.pallas_call(
        _rope_table_kernel,
        out_shape=(jax.ShapeDtypeStruct((t, LANES), F32),) * 2,
        grid=(t // ts,),
        in_specs=[
            pl.BlockSpec((ts, 1), lambda i: (i, 0)),
            pl.BlockSpec((1, LANES), lambda i: (0, 0)),
            pl.BlockSpec((1, LANES), lambda i: (0, 0)),
        ],
        out_specs=(pl.BlockSpec((ts, LANES), lambda i: (i, 0)),) * 2,
        compiler_params=pltpu.CompilerParams(dimension_semantics=("parallel",)),
        name="rope_tables",
    )(positions.reshape(t, 1), inv_lane, sign_lane)


def _rope(t, cos, sin_signed):
    lane = lax.broadcasted_iota(jnp.int32, (1, LANES), 1)
    low = (lane % MLA_ROPE_DIM) < (MLA_ROPE_DIM // 2)
    half = MLA_ROPE_DIM // 2
    swapped = jnp.where(low, pltpu.roll(t, LANES - half, 1), pltpu.roll(t, half, 1))
    return t * cos + swapped * sin_signed


def _in_proj_kernel(x_ref, mod_ref, g_ref, w_ref, gq_ref, wq_ref, gkv_ref, wkv_ref,
                    cos_ref, sin_ref,
                    qsb_ref, ksb_ref, vsb_ref, qm_ref, km_ref, vm_ref, sgs_ref, sgm_ref):
    mod = mod_ref[0]
    shift1 = mod[:, 0:D_MODEL]
    scale1 = mod[:, D_MODEL:2 * D_MODEL]
    h = _rms(x_ref[...], g_ref[...]) * (1.0 + scale1) + shift1
    hb = h.astype(BF16)

    qkv = _dot(hb, w_ref[:, OFF_QKV:OFF_QLAT])
    qsb_ref[...] = qkv[:, 0:SB_WIDTH].astype(BF16)
    ksb_ref[...] = qkv[:, SB_WIDTH:2 * SB_WIDTH].astype(BF16)
    vsb_ref[...] = qkv[:, 2 * SB_WIDTH:3 * SB_WIDTH].astype(BF16)

    gates = _dot(hb, w_ref[:, OFF_GATES:IN_W])
    sig = 1.0 / (1.0 + jnp.exp(-gates))
    sgs_ref[...] = sig[:, 0:D_MODEL].astype(BF16)
    sgm_ref[...] = sig[:, D_MODEL:2 * D_MODEL].astype(BF16)

    lat = _dot(hb, w_ref[:, OFF_QLAT:OFF_GATES])
    q_lat = lat[:, 0:MLA_Q_RANK]
    kv_lat = lat[:, MLA_Q_RANK:MLA_Q_RANK + MLA_KV_RANK]
    k_rope = lat[:, MLA_Q_RANK + MLA_KV_RANK:]
    cos = cos_ref[...]
    sin = sin_ref[...]

    q = _dot(_rms(q_lat, gq_ref[...]).astype(BF16), wq_ref[...]) * (MLA_QK_DIM ** -0.5)
    kv = _dot(_rms(kv_lat, gkv_ref[...]).astype(BF16), wkv_ref[...])
    k_pe = _rope(k_rope, cos, sin).astype(BF16)
    vm_ref[...] = kv[:, MLA_WIDTH:].astype(BF16)
    for p in range(N_PAIRS):
        lo = p * PAIR_W
        qm_ref[:, lo:lo + LANES] = q[:, lo:lo + LANES].astype(BF16)
        qm_ref[:, lo + LANES:lo + PAIR_W] = _rope(q[:, lo + LANES:lo + PAIR_W], cos, sin).astype(BF16)
        km_ref[:, lo:lo + LANES] = kv[:, p * LANES:(p + 1) * LANES].astype(BF16)
        km_ref[:, lo + LANES:lo + PAIR_W] = k_pe


def _in_proj_call(x, mod, g, w_in, gq, wq, gkv, wkv, cos_t, sin_t, seq):
    t = x.shape[0]
    tm = ROW_TILE
    tiles_per_seq = seq // tm
    row = lambda i: (i, 0)
    widths = (SB_WIDTH, SB_WIDTH, SB_WIDTH, MLA_QK_W, MLA_QK_W, MLA_WIDTH, D_MODEL, D_MODEL)
    return pl.pallas_call(
        _in_proj_kernel,
        out_shape=tuple(jax.ShapeDtypeStruct((t, w), BF16) for w in widths),
        grid=(t // tm,),
        in_specs=[
            pl.BlockSpec((tm, D_MODEL), row),
            pl.BlockSpec((1, 1, N_MOD * D_MODEL), lambda i: (i // tiles_per_seq, 0, 0)),
            _resident((1, D_MODEL)),
            _resident((D_MODEL, IN_W)),
            _resident((1, MLA_Q_RANK)),
            _resident((MLA_Q_RANK, MLA_QK_W)),
            _resident((1, MLA_KV_RANK)),
            _resident((MLA_KV_RANK, 2 * MLA_WIDTH)),
            pl.BlockSpec((tm, LANES), row),
            pl.BlockSpec((tm, LANES), row),
        ],
        out_specs=tuple(pl.BlockSpec((tm, w), row) for w in widths),
        compiler_params=pltpu.CompilerParams(
            dimension_semantics=("parallel",), vmem_limit_bytes=VMEM_LIMIT),
        name="in_proj",
    )(x, mod, g, w_in, gq, wq, gkv, wkv, cos_t, sin_t)


def _sb_attn_kernel(q_ref, k_ref, v_ref, o_ref, acc_ref, r_ref):
    i = pl.program_id(2)
    blk = ATTN_TILE
    lane = lax.broadcasted_iota(jnp.int32, (1, LANES), 1)
    first = lane < HEAD_DIM
    q = q_ref[...]
    zero = jnp.zeros_like(q)
    q_heads = (jnp.where(first, q, zero), jnp.where(first, zero, q))
    row = lax.broadcasted_iota(jnp.int32, (blk, blk), 0)
    col = lax.broadcasted_iota(jnp.int32, (blk, blk), 1)
    strict = col < row
    later = jnp.where(row > col, 1.0, 0.0).astype(BF16)

    acc_ref[...] = jnp.zeros_like(acc_ref)
    r_ref[...] = jnp.zeros_like(r_ref)

    def head(qh, k, v, diag):
        z = _dot_nt(qh, k)
        ls = jnp.minimum(z, 0.0) - jnp.log1p(jnp.exp(-jnp.abs(z)))
        lf = ls - z
        if diag:
            lf = jnp.where(strict, lf, 0.0)
        hi = lf.astype(BF16)
        lo = (lf - hi.astype(F32)).astype(BF16)
        cum = _dot(hi, later) + _dot(lo, later)
        w = jnp.exp(ls + cum)
        if diag:
            w = jnp.where(strict, w, 0.0)
        return _dot(w.astype(BF16), v), jnp.sum(lf, axis=1, keepdims=True)

    def block(j, diag):
        off = pl.multiple_of(j * blk, blk)
        k = k_ref[pl.ds(off, blk), :]
        v = v_ref[pl.ds(off, blk), :]
        pv_a, rs_a = head(q_heads[0], k, v, diag)
        pv_b, rs_b = head(q_heads[1], k, v, diag)
        r = r_ref[...]
        acc_ref[...] += jnp.exp(r) * jnp.where(first, pv_a, pv_b)
        r_ref[...] = r + jnp.where(first, rs_a, rs_b)

    block(i, True)

    def body(jj, carry):
        block(i - 1 - jj, False)
        return carry

    lax.fori_loop(0, i, body, 0)
    o_ref[...] = acc_ref[...].astype(o_ref.dtype)


def _sb_attn_call(q, k, v):
    b, s, _ = q.shape
    blk = ATTN_TILE
    q_spec = pl.BlockSpec((None, blk, LANES), lambda bi, p, i: (bi, i, p))
    kv_spec = pl.BlockSpec((None, s, LANES), lambda bi, p, i: (bi, 0, p))
    return pl.pallas_call(
        _sb_attn_kernel,
        out_shape=jax.ShapeDtypeStruct(q.shape, BF16),
        grid=(b, N_PAIRS, s // blk),
        in_specs=[q_spec, kv_spec, kv_spec],
        out_specs=q_spec,
        scratch_shapes=[pltpu.VMEM((blk, LANES), F32), pltpu.VMEM((blk, LANES), F32)],
        compiler_params=pltpu.CompilerParams(
            dimension_semantics=("parallel", "parallel", "arbitrary"),
            vmem_limit_bytes=VMEM_LIMIT),
        name="sb_attn",
    )(q, k, v)


def _mla_attn_kernel(q_ref, k_ref, v_ref, o_ref, acc_ref, m_ref, l_ref):
    i = pl.program_id(2)
    blk = ATTN_TILE
    lane = lax.broadcasted_iota(jnp.int32, (1, LANES), 1)
    first = lane < HEAD_DIM
    lane2 = lax.broadcasted_iota(jnp.int32, (1, PAIR_W), 1)
    own_a = (lane2 < HEAD_DIM) | ((lane2 >= LANES) & (lane2 < LANES + MLA_ROPE_DIM))
    own_b = ((lane2 >= HEAD_DIM) & (lane2 < LANES)) | (
        (lane2 >= LANES + MLA_ROPE_DIM) & (lane2 < LANES + 2 * MLA_ROPE_DIM))
    q = q_ref[...]
    zero = jnp.zeros_like(q)
    q_heads = (jnp.where(own_a, q, zero), jnp.where(own_b, q, zero))
    row = lax.broadcasted_iota(jnp.int32, (blk, blk), 0)
    col = lax.broadcasted_iota(jnp.int32, (blk, blk), 1)
    causal = col <= row

    acc_ref[...] = jnp.zeros_like(acc_ref)
    m_ref[...] = jnp.full_like(m_ref, -jnp.inf)
    l_ref[...] = jnp.zeros_like(l_ref)

    def head(h, k, v, diag):
        s = _dot_nt(q_heads[h], k)
        if diag:
            s = jnp.where(causal, s, -jnp.inf)
        m_prev = m_ref[h]
        m_new = jnp.maximum(m_prev, jnp.max(s, axis=1, keepdims=True))
        p = jnp.exp(s - m_new)
        alpha = jnp.exp(m_prev - m_new)
        l_ref[h] = alpha * l_ref[h] + jnp.sum(p, axis=1, keepdims=True)
        m_ref[h] = m_new
        return _dot(p.astype(BF16), v), alpha

    def block(j, diag):
        off = pl.multiple_of(j * blk, blk)
        k = k_ref[pl.ds(off, blk), :]
        v = v_ref[pl.ds(off, blk), :]
        pv_a, al_a = head(0, k, v, diag)
        pv_b, al_b = head(1, k, v, diag)
        acc_ref[...] = acc_ref[...] * jnp.where(first, al_a, al_b) + jnp.where(first, pv_a, pv_b)

    block(i, True)

    def body(j, carry):
        block(j, False)
        return carry

    lax.fori_loop(0, i, body, 0)
    o_ref[...] = (acc_ref[...] / jnp.where(first, l_ref[0], l_ref[1])).astype(o_ref.dtype)


def _mla_attn_call(q, k, v):
    b, s, _ = v.shape
    blk = ATTN_TILE
    return pl.pallas_call(
        _mla_attn_kernel,
        out_shape=jax.ShapeDtypeStruct(v.shape, BF16),
        grid=(b, N_PAIRS, s // blk),
        in_specs=[
            pl.BlockSpec((None, blk, PAIR_W), lambda bi, p, i: (bi, i, p)),
            pl.BlockSpec((None, s, PAIR_W), lambda bi, p, i: (bi, 0, p)),
            pl.BlockSpec((None, s, LANES), lambda bi, p, i: (bi, 0, p)),
        ],
        out_specs=pl.BlockSpec((None, blk, LANES), lambda bi, p, i: (bi, i, p)),
        scratch_shapes=[
            pltpu.VMEM((blk, LANES), F32),
            pltpu.VMEM((2, blk, 1), F32),
            pltpu.VMEM((2, blk, 1), F32),
        ],
        compiler_params=pltpu.CompilerParams(
            dimension_semantics=("parallel", "parallel", "arbitrary"),
            vmem_limit_bytes=VMEM_LIMIT),
        name="mla_attn",
    )(q, k, v)


def _post_kernel(x_ref, osb_ref, omla_ref, sgs_ref, sgm_ref, mod_ref,
                 wsb_ref, wmla_ref, wmix_ref, g2_ref, wup_ref, wdn_ref, gf_ref,
                 o_ref, *, final):
    mod = mod_ref[0]
    gate1 = mod[:, 2 * D_MODEL:3 * D_MODEL]
    shift2 = mod[:, 3 * D_MODEL:4 * D_MODEL]
    scale2 = mod[:, 4 * D_MODEL:5 * D_MODEL]
    gate2 = mod[:, 5 * D_MODEL:6 * D_MODEL]

    o_sb = _dot(osb_ref[...], wsb_ref[...])
    o_mla = _dot(omla_ref[...], wmla_ref[...])
    merged = sgs_ref[...].astype(F32) * o_sb + sgm_ref[...].astype(F32) * o_mla
    x = x_ref[...] + gate1 * _dot(merged.astype(BF16), wmix_ref[...])

    h = _rms(x, g2_ref[...]) * (1.0 + scale2) + shift2
    u = jnp.maximum(_dot(h.astype(BF16), wup_ref[...]), 0.0)
    x = x + gate2 * _dot((u * u).astype(BF16), wdn_ref[...])
    if final:
        x = _rms(x, gf_ref[...])
    o_ref[...] = x


def _post_call(x, o_sb, o_mla, sg_sb, sg_mla, mod, w_sb, w_mla, w_mix, g2, w_up, w_dn, g_final,
               seq, final):
    t = x.shape[0]
    tm = ROW_TILE
    tiles_per_seq = seq // tm
    row = lambda i: (i, 0)
    return pl.pallas_call(
        functools.partial(_post_kernel, final=final),
        out_shape=jax.ShapeDtypeStruct((t, D_MODEL), F32),
        grid=(t // tm,),
        in_specs=[
            pl.BlockSpec((tm, D_MODEL), row),
            pl.BlockSpec((tm, SB_WIDTH), row),
            pl.BlockSpec((tm, MLA_WIDTH), row),
            pl.BlockSpec((tm, D_MODEL), row),
            pl.BlockSpec((tm, D_MODEL), row),
            pl.BlockSpec((1, 1, N_MOD * D_MODEL), lambda i: (i // tiles_per_seq, 0, 0)),
            _resident((SB_WIDTH, D_MODEL)),
            _resident((MLA_WIDTH, D_MODEL)),
            _resident((D_MODEL, D_MODEL)),
            _resident((1, D_MODEL)),
            _resident((D_MODEL, D_FF)),
            _resident((D_FF, D_MODEL)),
            _resident((1, D_MODEL)),
        ],
        out_specs=pl.BlockSpec((tm, D_MODEL), row),
        compiler_params=pltpu.CompilerParams(
            dimension_semantics=("parallel",), vmem_limit_bytes=VMEM_LIMIT),
        name="post_final" if final else "post",
    )(x, o_sb, o_mla, sg_sb, sg_mla, mod, w_sb, w_mla, w_mix, g2, w_up, w_dn, g_final)


def _layout_w_in(w_in):
    depth, d, _ = w_in.shape
    q_sb = w_in[..., 0:SB_WIDTH] * (HEAD_DIM ** -0.5)
    mid = w_in[..., SB_WIDTH:OFF_KROPE]
    k_rope = w_in[..., OFF_KROPE:OFF_KROPE + MLA_ROPE_DIM]
    pad = jnp.zeros((depth, d, LANES - 2 * MLA_ROPE_DIM), w_in.dtype)
    gates = w_in[..., OFF_KROPE + MLA_ROPE_DIM:]
    return jnp.concatenate([q_sb, mid, k_rope, k_rope, pad, gates], axis=-1).astype(BF16)


def _layout_w_q_up(w_q_up):
    depth, r, _ = w_q_up.shape
    w = w_q_up.reshape(depth, r, N_PAIRS, 2, MLA_QK_DIM)
    nope = w[..., :HEAD_DIM].reshape(depth, r, N_PAIRS, LANES)
    rope = w[..., HEAD_DIM:].reshape(depth, r, N_PAIRS, 2 * MLA_ROPE_DIM)
    pad = jnp.zeros((depth, r, N_PAIRS, LANES - 2 * MLA_ROPE_DIM), w.dtype)
    return jnp.concatenate([nope, rope, pad], axis=-1).reshape(depth, r, MLA_QK_W).astype(BF16)


def _layout_w_kv_up(w_kv_up):
    depth, r, _ = w_kv_up.shape
    w = w_kv_up.reshape(depth, r, MLA_HEADS, 2 * HEAD_DIM)
    k = w[..., :HEAD_DIM].reshape(depth, r, MLA_WIDTH)
    v = w[..., HEAD_DIM:].reshape(depth, r, MLA_WIDTH)
    return jnp.concatenate([k, v], axis=-1).astype(BF16)


def kernel(x, c, positions, w_ada, b_ada, g_mix_norm, w_in, g_q_lat, w_q_up, g_kv_lat, w_kv_up,
           w_sb_out, w_mla_out, w_mix_out, g_mlp_norm, w_up, w_down, g_final):
    batch, seq, d = x.shape
    depth = w_in.shape[0]
    t = batch * seq

    w_in_l = _layout_w_in(w_in)
    w_q_l = _layout_w_q_up(w_q_up)
    w_kv_l = _layout_w_kv_up(w_kv_up)
    w_sb_b, w_mla_b, w_mix_b = (w.astype(BF16) for w in (w_sb_out, w_mla_out, w_mix_out))
    w_up_b, w_dn_b = w_up.astype(BF16), w_down.astype(BF16)

    mod_all = _mod_call(c, w_ada, b_ada).reshape(depth, batch, 1, N_MOD * d)
    cos_t, sin_t = _rope_tables(positions)

    xt = x.reshape(t, d)
    for l in range(depth):
        mod = mod_all[l]
        q_sb, k_sb, v_sb, q_m, k_m, v_m, sg_sb, sg_mla = _in_proj_call(
            xt, mod, g_mix_norm[l].reshape(1, d), w_in_l[l],
            g_q_lat[l].reshape(1, -1), w_q_l[l], g_kv_lat[l].reshape(1, -1), w_kv_l[l],
            cos_t, sin_t, seq)
        o_sb = _sb_attn_call(q_sb.reshape(batch, seq, -1), k_sb.reshape(batch, seq, -1),
                             v_sb.reshape(batch, seq, -1))
        o_mla = _mla_attn_call(q_m.reshape(batch, seq, -1), k_m.reshape(batch, seq, -1),
                               v_m.reshape(batch, seq, -1))
        xt = _post_call(xt, o_sb.reshape(t, -1), o_mla.reshape(t, -1), sg_sb, sg_mla, mod,
                        w_sb_b[l], w_mla_b[l], w_mix_b[l], g_mlp_norm[l].reshape(1, d),
                        w_up_b[l], w_dn_b[l], g_final.reshape(1, d), seq, l == depth - 1)
    return xt.reshape(batch, seq, d)
```

```python
import functools
import math

import jax
import jax.numpy as jnp
from jax import lax
from jax.experimental import pallas as pl
from jax.experimental.pallas import tpu as pltpu

D_MODEL = 1024
SB_HEADS = 8
HEAD_DIM = 64
SB_WIDTH = SB_HEADS * HEAD_DIM
MLA_HEADS = 8
MLA_ROPE_DIM = 32
MLA_QK_DIM = HEAD_DIM + MLA_ROPE_DIM
MLA_Q_RANK = 384
MLA_KV_RANK = 256
MLA_WIDTH = MLA_HEADS * HEAD_DIM
D_FF = 4 * D_MODEL
ROPE_THETA = 10000.0
NORM_EPS = 1e-6
N_MOD = 6
LOG2E = math.log2(math.e)

LANES = 128
PAIR_W = 2 * LANES
N_PAIRS = SB_HEADS // 2
MLA_QK_W = N_PAIRS * PAIR_W

OFF_QKV = 0
OFF_QLAT = 3 * SB_WIDTH
OFF_KVLAT = OFF_QLAT + MLA_Q_RANK
OFF_KROPE = OFF_KVLAT + MLA_KV_RANK
OFF_GATES = OFF_KROPE + LANES
IN_W = OFF_GATES + 2 * D_MODEL

ROW_TILE = 256
ATTN_TILE = 256
N_STREAMS = 2
VMEM_LIMIT = 56 * 1024 * 1024
MASKED = -1e30

F32 = jnp.float32
BF16 = jnp.bfloat16


def _resident(shape):
    nd = len(shape)
    return pl.BlockSpec(shape, lambda *_: (0,) * nd, pipeline_mode=pl.Buffered(1))


def _rms(x, g):
    return x * lax.rsqrt(jnp.mean(x * x, axis=-1, keepdims=True) + NORM_EPS) * g


def _dot(a, b):
    return jnp.dot(a, b, preferred_element_type=F32)


def _dot_nt(a, b):
    return lax.dot_general(a, b, (((1,), (1,)), ((), ())), preferred_element_type=F32)


def _mod_kernel(c_ref, w_ref, b_ref, o_ref):
    c = c_ref[...]
    c_act = c * (1.0 / (1.0 + jnp.exp(-c)))
    o_ref[0] = _dot(c_act.astype(BF16), w_ref[0].astype(BF16)) + b_ref[0]


def _mod_call(c, w_ada, b_ada):
    depth, d, n = w_ada.shape
    batch = c.shape[0]
    tn = 1536
    return pl.pallas_call(
        _mod_kernel,
        out_shape=jax.ShapeDtypeStruct((depth, batch, n), F32),
        grid=(depth, n // tn),
        in_specs=[
            pl.BlockSpec((batch, d), lambda l, j: (0, 0)),
            pl.BlockSpec((1, d, tn), lambda l, j: (l, 0, j)),
            pl.BlockSpec((1, 1, tn), lambda l, j: (l, 0, j)),
        ],
        out_specs=pl.BlockSpec((1, batch, tn), lambda l, j: (l, 0, j)),
        compiler_params=pltpu.CompilerParams(
            dimension_semantics=("parallel", "parallel"), vmem_limit_bytes=VMEM_LIMIT),
        name="adaln_mod",
    )(c, w_ada, b_ada.reshape(depth, 1, n))


def _rope_table_kernel(pos_ref, inv_ref, sign_ref, cos_ref, sin_ref):
    ang = pos_ref[...].astype(F32) * inv_ref[...]
    cos_ref[...] = jnp.cos(ang)
    sin_ref[...] = jnp.sin(ang) * sign_ref[...]


def _rope_tables(positions):
    t = positions.size
    half = MLA_ROPE_DIM // 2
    inv_freq = 1.0 / (ROPE_THETA ** (jnp.arange(0, MLA_ROPE_DIM, 2, dtype=F32) / MLA_ROPE_DIM))
    lane = jnp.arange(LANES)
    inv_lane = inv_freq[lane % half].reshape(1, LANES)
    sign_lane = jnp.where((lane % MLA_ROPE_DIM) < half, -1.0, 1.0).astype(F32).reshape(1, LANES)
    ts = 2048
    return pl.pallas_call(
        _rope_table_kernel,
        out_shape=(jax.ShapeDtypeStruct((t, LANES), F32),) * 2,
        grid=(t // ts,),
        in_specs=[
            pl.BlockSpec((ts, 1), lambda i: (i, 0)),
            pl.BlockSpec((1, LANES), lambda i: (0, 0)),
            pl.BlockSpec((1, LANES), lambda i: (0, 0)),
        ],
        out_specs=(pl.BlockSpec((ts, LANES), lambda i: (i, 0)),) * 2,
        compiler_params=pltpu.CompilerParams(dimension_semantics=("parallel",)),
        name="rope_tables",
    )(positions.reshape(t, 1), inv_lane, sign_lane)


def _rope(t, cos, sin_signed):
    lane = lax.broadcasted_iota(jnp.int32, (1, LANES), 1)
    low = (lane % MLA_ROPE_DIM) < (MLA_ROPE_DIM // 2)
    half = MLA_ROPE_DIM // 2
    swapped = jnp.where(low, pltpu.roll(t, LANES - half, 1), pltpu.roll(t, half, 1))
    return t * cos + swapped * sin_signed


def _in_proj_kernel(x_ref, mod_ref, g_ref, w_ref, gq_ref, wq_ref, gkv_ref, wkv_ref,
                    cos_ref, sin_ref,
                    qsb_ref, ksb_ref, vsb_ref, qm_ref, km_ref, vm_ref, sgs_ref, sgm_ref):
    mod = mod_ref[0]
    shift1 = mod[:, 0:D_MODEL]
    scale1 = mod[:, D_MODEL:2 * D_MODEL]
    h = _rms(x_ref[...], g_ref[...]) * (1.0 + scale1) + shift1
    hb = h.astype(BF16)

    qkv = _dot(hb, w_ref[:, OFF_QKV:OFF_QLAT])
    qsb_ref[...] = (qkv[:, 0:SB_WIDTH] * (HEAD_DIM ** -0.5 * LOG2E)).astype(BF16)
    ksb_ref[...] = qkv[:, SB_WIDTH:2 * SB_WIDTH].astype(BF16)
    vsb_ref[...] = qkv[:, 2 * SB_WIDTH:3 * SB_WIDTH].astype(BF16)

    gates = _dot(hb, w_ref[:, OFF_GATES:IN_W])
    sig = 1.0 / (1.0 + jnp.exp(-gates))
    sgs_ref[...] = sig[:, 0:D_MODEL].astype(BF16)
    sgm_ref[...] = sig[:, D_MODEL:2 * D_MODEL].astype(BF16)

    lat = _dot(hb, w_ref[:, OFF_QLAT:OFF_GATES])
    q_lat = lat[:, 0:MLA_Q_RANK]
    kv_lat = lat[:, MLA_Q_RANK:MLA_Q_RANK + MLA_KV_RANK]
    k_rope = lat[:, MLA_Q_RANK + MLA_KV_RANK:]
    cos = cos_ref[...]
    sin = sin_ref[...]

    q = _dot(_rms(q_lat, gq_ref[...]).astype(BF16), wq_ref[...]) * (MLA_QK_DIM ** -0.5 * LOG2E)
    kv = _dot(_rms(kv_lat, gkv_ref[...]).astype(BF16), wkv_ref[...])
    k_pe = _rope(k_rope, cos, sin).astype(BF16)
    vm_ref[...] = kv[:, MLA_WIDTH:].astype(BF16)
    for p in range(N_PAIRS):
        lo = p * PAIR_W
        qm_ref[:, lo:lo + LANES] = q[:, lo:lo + LANES].astype(BF16)
        qm_ref[:, lo + LANES:lo + PAIR_W] = _rope(q[:, lo + LANES:lo + PAIR_W], cos, sin).astype(BF16)
        km_ref[:, lo:lo + LANES] = kv[:, p * LANES:(p + 1) * LANES].astype(BF16)
        km_ref[:, lo + LANES:lo + PAIR_W] = k_pe


def _in_proj_call(x, mod, g, w_in, gq, wq, gkv, wkv, cos_t, sin_t, seq):
    t = x.shape[0]
    tm = ROW_TILE
    tiles_per_seq = seq // tm
    row = lambda i: (i, 0)
    widths = (SB_WIDTH, SB_WIDTH, SB_WIDTH, MLA_QK_W, MLA_QK_W, MLA_WIDTH, D_MODEL, D_MODEL)
    return pl.pallas_call(
        _in_proj_kernel,
        out_shape=tuple(jax.ShapeDtypeStruct((t, w), BF16) for w in widths),
        grid=(t // tm,),
        in_specs=[
            pl.BlockSpec((tm, D_MODEL), row),
            pl.BlockSpec((1, 1, N_MOD * D_MODEL), lambda i: (i // tiles_per_seq, 0, 0)),
            _resident((1, D_MODEL)),
            _resident((D_MODEL, IN_W)),
            _resident((1, MLA_Q_RANK)),
            _resident((MLA_Q_RANK, MLA_QK_W)),
            _resident((1, MLA_KV_RANK)),
            _resident((MLA_KV_RANK, 2 * MLA_WIDTH)),
            pl.BlockSpec((tm, LANES), row),
            pl.BlockSpec((tm, LANES), row),
        ],
        out_specs=tuple(pl.BlockSpec((tm, w), row) for w in widths),
        compiler_params=pltpu.CompilerParams(
            dimension_semantics=("parallel",), vmem_limit_bytes=VMEM_LIMIT),
        name="in_proj",
    )(x, mod, g, w_in, gq, wq, gkv, wkv, cos_t, sin_t)


def _stream_blocks(nq):
    sets = [[] for _ in range(N_STREAMS)]
    load = [0] * N_STREAMS
    for i in range(nq - 1, -1, -1):
        s = min(range(N_STREAMS), key=lambda a: (load[a], len(sets[a])))
        sets[s].append(i)
        load[s] += i
    assert len(set(load)) == 1 and len({len(st) for st in sets}) == 1, (sets, load)
    owner = {i: s for s, st in enumerate(sets) for i in st}
    diag = [[(i, i) for i in st] for st in sets]
    offd = [[(i, j) for i in st for j in range(i - 1, -1, -1)] for st in sets]
    blocks = [b for group in zip(*diag) for b in group] + [b for group in zip(*offd) for b in group]
    return blocks, owner


def _block_offset(i):
    return i * ATTN_TILE if isinstance(i, int) else pl.multiple_of(i * ATTN_TILE, ATTN_TILE)


def _run_pipeline(stages, it_ref, jt_ref, blocks, base, n, diag):
    depth = len(stages)
    assert n >= depth

    def static_step(t):
        for s in range(N_STREAMS):
            for k in range(depth - 1, -1, -1):
                if 0 <= t - k < n:
                    stages[k](s, *blocks[base + (t - k) * N_STREAMS + s], diag)

    for t in range(depth - 1):
        static_step(t)

    def body(t, carry):
        for s in range(N_STREAMS):
            for k in range(depth - 1, -1, -1):
                e = base + (t - k) * N_STREAMS + s
                stages[k](s, it_ref[e], jt_ref[e], diag)
        return carry

    lax.fori_loop(depth - 1, n, body, 0)
    for t in range(n, n + depth - 1):
        static_step(t)


def _attn_tables(blocks):
    return (jnp.asarray([b[0] for b in blocks], jnp.int32),
            jnp.asarray([b[1] for b in blocks], jnp.int32))


def _sb_attn_kernel(it_ref, jt_ref, q_ref, k_ref, v_ref, o_ref, later_ref, *scratch,
                    nq, blocks, owner):
    blk = ATTN_TILE
    per_stream = [scratch[7 * s:7 * s + 7] for s in range(N_STREAMS)]
    acc_refs, r_refs, ls_refs, hl_refs, rs0_refs, rs1_refs, w_refs = zip(*per_stream)
    lane = lax.broadcasted_iota(jnp.int32, (1, LANES), 1)
    first = lane < HEAD_DIM
    row = lax.broadcasted_iota(jnp.int32, (blk, blk), 0)
    col = lax.broadcasted_iota(jnp.int32, (blk, blk), 1)
    row2 = lax.broadcasted_iota(jnp.int32, (2 * blk, blk), 0) & (blk - 1)
    col2 = lax.broadcasted_iota(jnp.int32, (2 * blk, blk), 1)
    strict = col2 < row2
    later = jnp.where(row > col, 1.0, 0.0).astype(BF16)
    later_ref[0:blk, :] = later
    later_ref[blk:2 * blk, :] = later

    def scores(s, i, j, diag):
        q = q_ref[pl.ds(_block_offset(i), blk), :]
        zero = jnp.zeros_like(q)
        q2 = jnp.concatenate([jnp.where(first, q, zero), jnp.where(first, zero, q)], axis=0)
        z = _dot_nt(q2, k_ref[pl.ds(_block_offset(j), blk), :])
        neg_part = jnp.minimum(z, 0.0)
        neg_relu = neg_part - z
        softplus = jnp.log2(1.0 + jnp.exp2(neg_part + neg_relu))
        ls = neg_part - softplus
        lf = neg_relu - softplus
        if diag:
            ls = jnp.where(strict, ls, MASKED)
            lf = jnp.where(strict, lf, 0.0)
        hi = lf.astype(BF16)
        lo = (lf - hi.astype(F32)).astype(BF16)
        rs = jnp.sum(lf, axis=1, keepdims=True)
        ls_refs[s][...] = ls
        hl_refs[s][:, 0:blk] = hi
        hl_refs[s][:, blk:2 * blk] = lo
        rs0_refs[s][...] = jnp.where(first, rs[0:blk], rs[blk:2 * blk])

    def weights(s, i, j, diag):
        cum = _dot(hl_refs[s][...], later_ref[...])
        w_refs[s][...] = jnp.exp2(ls_refs[s][...] + cum).astype(BF16)
        rs1_refs[s][...] = rs0_refs[s][...]

    def accumulate(s, i, j, diag):
        pv = _dot(w_refs[s][...], v_ref[pl.ds(_block_offset(j), blk), :])
        pv = jnp.where(first, pv[0:blk], pv[blk:2 * blk])
        rs = rs1_refs[s][...]
        if diag:
            acc_refs[s][i] = pv
            r_refs[s][i] = rs
        else:
            r = r_refs[s][i]
            acc_refs[s][i] = acc_refs[s][i] + jnp.exp2(r) * pv
            r_refs[s][i] = r + rs

    stages = (scores, weights, accumulate)
    n_diag = nq // N_STREAMS
    _run_pipeline(stages, it_ref, jt_ref, blocks, 0, n_diag, True)
    _run_pipeline(stages, it_ref, jt_ref, blocks, nq, (len(blocks) - nq) // N_STREAMS, False)
    for i in range(nq):
        o_ref[i * blk:(i + 1) * blk, :] = acc_refs[owner[i]][i].astype(o_ref.dtype)


def _sb_attn_call(q, k, v):
    b, s, width = q.shape
    blk = ATTN_TILE
    nq = s // blk
    blocks, owner = _stream_blocks(nq)
    spec = pl.BlockSpec((None, s, LANES), lambda bi, p, *_: (bi, 0, p))
    grid_spec = pltpu.PrefetchScalarGridSpec(
        num_scalar_prefetch=2,
        grid=(b, width // LANES),
        in_specs=[spec, spec, spec],
        out_specs=spec,
        scratch_shapes=[pltpu.VMEM((2 * blk, blk), BF16)] + N_STREAMS * [
            pltpu.VMEM((nq, blk, LANES), F32),
            pltpu.VMEM((nq, blk, LANES), F32),
            pltpu.VMEM((2 * blk, blk), F32),
            pltpu.VMEM((2 * blk, 2 * blk), BF16),
            pltpu.VMEM((blk, LANES), F32),
            pltpu.VMEM((blk, LANES), F32),
            pltpu.VMEM((2 * blk, blk), BF16),
        ],
    )
    return pl.pallas_call(
        functools.partial(_sb_attn_kernel, nq=nq, blocks=blocks, owner=owner),
        out_shape=jax.ShapeDtypeStruct(q.shape, BF16),
        grid_spec=grid_spec,
        compiler_params=pltpu.CompilerParams(
            dimension_semantics=("parallel", "parallel"), vmem_limit_bytes=VMEM_LIMIT),
        name="sb_attn",
    )(*_attn_tables(blocks), q, k, v)


def _mla_attn_kernel(it_ref, jt_ref, q_ref, k_ref, v_ref, o_ref, vext_ref, *scratch,
                     nq, blocks, owner):
    blk = ATTN_TILE
    per_stream = [scratch[4 * s:4 * s + 4] for s in range(N_STREAMS)]
    acc_refs, m_refs, alpha_refs, p_refs = zip(*per_stream)
    lane = lax.broadcasted_iota(jnp.int32, (1, LANES), 1)
    first = lane < HEAD_DIM
    lane2 = lax.broadcasted_iota(jnp.int32, (1, PAIR_W), 1)
    own_a = (lane2 < HEAD_DIM) | ((lane2 >= LANES) & (lane2 < LANES + MLA_ROPE_DIM))
    own_b = ((lane2 >= HEAD_DIM) & (lane2 < LANES)) | (
        (lane2 >= LANES + MLA_ROPE_DIM) & (lane2 < LANES + 2 * MLA_ROPE_DIM))
    row2 = lax.broadcasted_iota(jnp.int32, (2 * blk, blk), 0) & (blk - 1)
    col2 = lax.broadcasted_iota(jnp.int32, (2 * blk, blk), 1)
    causal = col2 <= row2
    vext_ref[:, 0:LANES] = v_ref[...]
    vext_ref[:, LANES:PAIR_W] = jnp.ones((v_ref.shape[0], LANES), BF16)

    def probabilities(s, i, j, diag):
        q = q_ref[pl.ds(_block_offset(i), blk), :]
        zero = jnp.zeros_like(q)
        q2 = jnp.concatenate([jnp.where(own_a, q, zero), jnp.where(own_b, q, zero)], axis=0)
        sc = _dot_nt(q2, k_ref[pl.ds(_block_offset(j), blk), :])
        if diag:
            sc = jnp.where(causal, sc, MASKED)
        row_max = jnp.max(sc, axis=1, keepdims=True)
        if diag:
            m_new = jnp.broadcast_to(row_max, (2 * blk, LANES))
        else:
            m_old = m_refs[s][i]
            m_new = jnp.maximum(m_old, row_max)
            alpha_refs[s][...] = jnp.exp2(m_old - m_new)
        m_refs[s][i] = m_new
        p_refs[s][...] = jnp.exp2(sc - jnp.concatenate([m_new, m_new], axis=1)).astype(BF16)

    def accumulate(s, i, j, diag):
        pv = _dot(p_refs[s][...], vext_ref[pl.ds(_block_offset(j), blk), :])
        if diag:
            acc_refs[s][i] = pv
        else:
            alpha = alpha_refs[s][...]
            acc_refs[s][i] = acc_refs[s][i] * jnp.concatenate([alpha, alpha], axis=1) + pv

    stages = (probabilities, accumulate)
    n_diag = nq // N_STREAMS
    _run_pipeline(stages, it_ref, jt_ref, blocks, 0, n_diag, True)
    _run_pipeline(stages, it_ref, jt_ref, blocks, nq, (len(blocks) - nq) // N_STREAMS, False)
    for i in range(nq):
        a = acc_refs[owner[i]][i]
        num = jnp.where(first, a[0:blk, 0:LANES], a[blk:2 * blk, 0:LANES])
        den = jnp.where(first, a[0:blk, LANES:PAIR_W], a[blk:2 * blk, LANES:PAIR_W])
        o_ref[i * blk:(i + 1) * blk, :] = (num / den).astype(o_ref.dtype)


def _mla_attn_call(q, k, v):
    b, s, width = v.shape
    blk = ATTN_TILE
    nq = s // blk
    blocks, owner = _stream_blocks(nq)
    qk_spec = pl.BlockSpec((None, s, PAIR_W), lambda bi, p, *_: (bi, 0, p))
    v_spec = pl.BlockSpec((None, s, LANES), lambda bi, p, *_: (bi, 0, p))
    grid_spec = pltpu.PrefetchScalarGridSpec(
        num_scalar_prefetch=2,
        grid=(b, width // LANES),
        in_specs=[qk_spec, qk_spec, v_spec],
        out_specs=v_spec,
        scratch_shapes=[pltpu.VMEM((s, PAIR_W), BF16)] + N_STREAMS * [
            pltpu.VMEM((nq, 2 * blk, PAIR_W), F32),
            pltpu.VMEM((nq, 2 * blk, LANES), F32),
            pltpu.VMEM((2 * blk, LANES), F32),
            pltpu.VMEM((2 * blk, blk), BF16),
        ],
    )
    return pl.pallas_call(
        functools.partial(_mla_attn_kernel, nq=nq, blocks=blocks, owner=owner),
        out_shape=jax.ShapeDtypeStruct(v.shape, BF16),
        grid_spec=grid_spec,
        compiler_params=pltpu.CompilerParams(
            dimension_semantics=("parallel", "parallel"), vmem_limit_bytes=VMEM_LIMIT),
        name="mla_attn",
    )(*_attn_tables(blocks), q, k, v)


def _post_kernel(x_ref, osb_ref, omla_ref, sgs_ref, sgm_ref, mod_ref,
                 wsb_ref, wmla_ref, wmix_ref, g2_ref, wup_ref, wdn_ref, gf_ref,
                 o_ref, *, final):
    mod = mod_ref[0]
    gate1 = mod[:, 2 * D_MODEL:3 * D_MODEL]
    shift2 = mod[:, 3 * D_MODEL:4 * D_MODEL]
    scale2 = mod[:, 4 * D_MODEL:5 * D_MODEL]
    gate2 = mod[:, 5 * D_MODEL:6 * D_MODEL]

    o_sb = _dot(osb_ref[...], wsb_ref[...])
    o_mla = _dot(omla_ref[...], wmla_ref[...])
    merged = sgs_ref[...].astype(F32) * o_sb + sgm_ref[...].astype(F32) * o_mla
    x = x_ref[...] + gate1 * _dot(merged.astype(BF16), wmix_ref[...])

    h = _rms(x, g2_ref[...]) * (1.0 + scale2) + shift2
    u = jnp.maximum(_dot(h.astype(BF16), wup_ref[...]), 0.0)
    x = x + gate2 * _dot((u * u).astype(BF16), wdn_ref[...])
    if final:
        x = _rms(x, gf_ref[...])
    o_ref[...] = x


def _post_call(x, o_sb, o_mla, sg_sb, sg_mla, mod, w_sb, w_mla, w_mix, g2, w_up, w_dn, g_final,
               seq, final):
    t = x.shape[0]
    tm = ROW_TILE
    tiles_per_seq = seq // tm
    row = lambda i: (i, 0)
    return pl.pallas_call(
        functools.partial(_post_kernel, final=final),
        out_shape=jax.ShapeDtypeStruct((t, D_MODEL), F32),
        grid=(t // tm,),
        in_specs=[
            pl.BlockSpec((tm, D_MODEL), row),
            pl.BlockSpec((tm, SB_WIDTH), row),
            pl.BlockSpec((tm, MLA_WIDTH), row),
            pl.BlockSpec((tm, D_MODEL), row),
            pl.BlockSpec((tm, D_MODEL), row),
            pl.BlockSpec((1, 1, N_MOD * D_MODEL), lambda i: (i // tiles_per_seq, 0, 0)),
            _resident((SB_WIDTH, D_MODEL)),
            _resident((MLA_WIDTH, D_MODEL)),
            _resident((D_MODEL, D_MODEL)),
            _resident((1, D_MODEL)),
            _resident((D_MODEL, D_FF)),
            _resident((D_FF, D_MODEL)),
            _resident((1, D_MODEL)),
        ],
        out_specs=pl.BlockSpec((tm, D_MODEL), row),
        compiler_params=pltpu.CompilerParams(
            dimension_semantics=("parallel",), vmem_limit_bytes=VMEM_LIMIT),
        name="post_final" if final else "post",
    )(x, o_sb, o_mla, sg_sb, sg_mla, mod, w_sb, w_mla, w_mix, g2, w_up, w_dn, g_final)


def _layout_w_in(w_in):
    depth, d, _ = w_in.shape
    head = w_in[..., 0:OFF_KROPE]
    k_rope = w_in[..., OFF_KROPE:OFF_KROPE + MLA_ROPE_DIM]
    pad = jnp.zeros((depth, d, LANES - 2 * MLA_ROPE_DIM), w_in.dtype)
    gates = w_in[..., OFF_KROPE + MLA_ROPE_DIM:]
    return jnp.concatenate([head, k_rope, k_rope, pad, gates], axis=-1).astype(BF16)


def _layout_w_q_up(w_q_up):
    depth, r, _ = w_q_up.shape
    w = w_q_up.reshape(depth, r, N_PAIRS, 2, MLA_QK_DIM)
    nope = w[..., :HEAD_DIM].reshape(depth, r, N_PAIRS, LANES)
    rope = w[..., HEAD_DIM:].reshape(depth, r, N_PAIRS, 2 * MLA_ROPE_DIM)
    pad = jnp.zeros((depth, r, N_PAIRS, LANES - 2 * MLA_ROPE_DIM), w.dtype)
    return jnp.concatenate([nope, rope, pad], axis=-1).reshape(depth, r, MLA_QK_W).astype(BF16)


def _layout_w_kv_up(w_kv_up):
    depth, r, _ = w_kv_up.shape
    w = w_kv_up.reshape(depth, r, MLA_HEADS, 2 * HEAD_DIM)
    k = w[..., :HEAD_DIM].reshape(depth, r, MLA_WIDTH)
    v = w[..., HEAD_DIM:].reshape(depth, r, MLA_WIDTH)
    return jnp.concatenate([k, v], axis=-1).astype(BF16)


def kernel(x, c, positions, w_ada, b_ada, g_mix_norm, w_in, g_q_lat, w_q_up, g_kv_lat, w_kv_up,
           w_sb_out, w_mla_out, w_mix_out, g_mlp_norm, w_up, w_down, g_final):
    batch, seq, d = x.shape
    depth = w_in.shape[0]
    t = batch * seq

    w_in_l = _layout_w_in(w_in)
    w_q_l = _layout_w_q_up(w_q_up)
    w_kv_l = _layout_w_kv_up(w_kv_up)
    w_sb_b, w_mla_b, w_mix_b = (w.astype(BF16) for w in (w_sb_out, w_mla_out, w_mix_out))
    w_up_b, w_dn_b = w_up.astype(BF16), w_down.astype(BF16)

    mod_all = _mod_call(c, w_ada, b_ada).reshape(depth, batch, 1, N_MOD * d)
    cos_t, sin_t = _rope_tables(positions)

    xt = x.reshape(t, d)
    for l in range(depth):
        mod = mod_all[l]
        q_sb, k_sb, v_sb, q_m, k_m, v_m, sg_sb, sg_mla = _in_proj_call(
            xt, mod, g_mix_norm[l].reshape(1, d), w_in_l[l],
            g_q_lat[l].reshape(1, -1), w_q_l[l], g_kv_lat[l].reshape(1, -1), w_kv_l[l],
            cos_t, sin_t, seq)
        o_sb = _sb_attn_call(q_sb.reshape(batch, seq, -1), k_sb.reshape(batch, seq, -1),
                             v_sb.reshape(batch, seq, -1))
        o_mla = _mla_attn_call(q_m.reshape(batch, seq, -1), k_m.reshape(batch, seq, -1),
                               v_m.reshape(batch, seq, -1))
        xt = _post_call(xt, o_sb.reshape(t, -1), o_mla.reshape(t, -1), sg_sb, sg_mla, mod,
                        w_sb_b[l], w_mla_b[l], w_mix_b[l], g_mlp_norm[l].reshape(1, d),
                        w_up_b[l], w_dn_b[l], g_final.reshape(1, d), seq, l == depth - 1)
    return xt.reshape(batch, seq, d)
```

```python
import functools
import math

import jax
import jax.numpy as jnp
from jax import lax
from jax.experimental import pallas as pl
from jax.experimental.pallas import tpu as pltpu

D_MODEL = 1024
SB_HEADS = 8
HEAD_DIM = 64
SB_WIDTH = SB_HEADS * HEAD_DIM
MLA_HEADS = 8
MLA_ROPE_DIM = 32
MLA_QK_DIM = HEAD_DIM + MLA_ROPE_DIM
MLA_Q_RANK = 384
MLA_KV_RANK = 256
MLA_WIDTH = MLA_HEADS * HEAD_DIM
D_FF = 4 * D_MODEL
ROPE_THETA = 10000.0
NORM_EPS = 1e-6
N_MOD = 6
LOG2E = math.log2(math.e)

LANES = 128
PAIR_W = 2 * LANES
N_PAIRS = SB_HEADS // 2
MLA_QK_W = N_PAIRS * PAIR_W

OFF_QKV = 0
OFF_QLAT = 3 * SB_WIDTH
OFF_KVLAT = OFF_QLAT + MLA_Q_RANK
OFF_KROPE = OFF_KVLAT + MLA_KV_RANK
OFF_GATES = OFF_KROPE + LANES
IN_W = OFF_GATES + 2 * D_MODEL

ROW_TILE = 512
ATTN_TILE = 256
VMEM_LIMIT = 56 * 1024 * 1024
MASKED = -1e30

F32 = jnp.float32
BF16 = jnp.bfloat16


def _resident(shape):
    nd = len(shape)
    return pl.BlockSpec(shape, lambda *_: (0,) * nd, pipeline_mode=pl.Buffered(1))


def _rms(x, g):
    return x * lax.rsqrt(jnp.mean(x * x, axis=-1, keepdims=True) + NORM_EPS) * g


def _dot(a, b):
    return jnp.dot(a, b, preferred_element_type=F32)


def _dot_nt(a, b):
    return lax.dot_general(a, b, (((1,), (1,)), ((), ())), preferred_element_type=F32)


def _mod_kernel(c_ref, w_ref, b_ref, o_ref):
    c = c_ref[...]
    c_act = c * (1.0 / (1.0 + jnp.exp(-c)))
    o_ref[0] = _dot(c_act.astype(BF16), w_ref[0].astype(BF16)) + b_ref[0]


def _mod_call(c, w_ada, b_ada):
    depth, d, n = w_ada.shape
    batch = c.shape[0]
    tn = 1536
    return pl.pallas_call(
        _mod_kernel,
        out_shape=jax.ShapeDtypeStruct((depth, batch, n), F32),
        grid=(depth, n // tn),
        in_specs=[
            pl.BlockSpec((batch, d), lambda l, j: (0, 0)),
            pl.BlockSpec((1, d, tn), lambda l, j: (l, 0, j)),
            pl.BlockSpec((1, 1, tn), lambda l, j: (l, 0, j)),
        ],
        out_specs=pl.BlockSpec((1, batch, tn), lambda l, j: (l, 0, j)),
        compiler_params=pltpu.CompilerParams(
            dimension_semantics=("parallel", "parallel"), vmem_limit_bytes=VMEM_LIMIT),
        name="adaln_mod",
    )(c, w_ada, b_ada.reshape(depth, 1, n))


def _rope_table_kernel(pos_ref, inv_ref, sign_ref, cos_ref, sin_ref):
    ang = pos_ref[...].astype(F32) * inv_ref[...]
    cos_ref[...] = jnp.cos(ang)
    sin_ref[...] = jnp.sin(ang) * sign_ref[...]


def _rope_tables(positions):
    t = positions.size
    half = MLA_ROPE_DIM // 2
    inv_freq = 1.0 / (ROPE_THETA ** (jnp.arange(0, MLA_ROPE_DIM, 2, dtype=F32) / MLA_ROPE_DIM))
    lane = jnp.arange(LANES)
    inv_lane = inv_freq[lane % half].reshape(1, LANES)
    sign_lane = jnp.where((lane % MLA_ROPE_DIM) < half, -1.0, 1.0).astype(F32).reshape(1, LANES)
    ts = 2048
    return pl.pallas_call(
        _rope_table_kernel,
        out_shape=(jax.ShapeDtypeStruct((t, LANES), F32),) * 2,
        grid=(t // ts,),
        in_specs=[
            pl.BlockSpec((ts, 1), lambda i: (i, 0)),
            pl.BlockSpec((1, LANES), lambda i: (0, 0)),
            pl.BlockSpec((1, LANES), lambda i: (0, 0)),
        ],
        out_specs=(pl.BlockSpec((ts, LANES), lambda i: (i, 0)),) * 2,
        compiler_params=pltpu.CompilerParams(dimension_semantics=("parallel",)),
        name="rope_tables",
    )(positions.reshape(t, 1), inv_lane, sign_lane)


def _rope(t, cos, sin_signed):
    lane = lax.broadcasted_iota(jnp.int32, (1, LANES), 1)
    low = (lane % MLA_ROPE_DIM) < (MLA_ROPE_DIM // 2)
    half = MLA_ROPE_DIM // 2
    swapped = jnp.where(low, pltpu.roll(t, LANES - half, 1), pltpu.roll(t, half, 1))
    return t * cos + swapped * sin_signed


def _in_proj_kernel(x_ref, mod_ref, g_ref, w_ref, gq_ref, wq_ref, gkv_ref, wkv_ref,
                    cos_ref, sin_ref,
                    qsb_ref, ksb_ref, vsb_ref, qm_ref, km_ref, vm_ref, sgs_ref, sgm_ref):
    mod = mod_ref[0]
    shift1 = mod[:, 0:D_MODEL]
    scale1 = mod[:, D_MODEL:2 * D_MODEL]
    h = _rms(x_ref[...], g_ref[...]) * (1.0 + scale1) + shift1
    hb = h.astype(BF16)

    qkv = _dot(hb, w_ref[:, OFF_QKV:OFF_QLAT])
    for p in range(N_PAIRS):
        lo = p * LANES
        qsb_ref[p] = (qkv[:, lo:lo + LANES] * (HEAD_DIM ** -0.5 * LOG2E)).astype(BF16)
        ksb_ref[p] = qkv[:, SB_WIDTH + lo:SB_WIDTH + lo + LANES].astype(BF16)
        vsb_ref[p] = qkv[:, 2 * SB_WIDTH + lo:2 * SB_WIDTH + lo + LANES].astype(BF16)

    gates = _dot(hb, w_ref[:, OFF_GATES:IN_W])
    sig = 1.0 / (1.0 + jnp.exp(-gates))
    sgs_ref[...] = sig[:, 0:D_MODEL].astype(BF16)
    sgm_ref[...] = sig[:, D_MODEL:2 * D_MODEL].astype(BF16)

    lat = _dot(hb, w_ref[:, OFF_QLAT:OFF_GATES])
    q_lat = lat[:, 0:MLA_Q_RANK]
    kv_lat = lat[:, MLA_Q_RANK:MLA_Q_RANK + MLA_KV_RANK]
    k_rope = lat[:, MLA_Q_RANK + MLA_KV_RANK:]
    cos = cos_ref[...]
    sin = sin_ref[...]

    q = _dot(_rms(q_lat, gq_ref[...]).astype(BF16), wq_ref[...]) * (MLA_QK_DIM ** -0.5 * LOG2E)
    kv = _dot(_rms(kv_lat, gkv_ref[...]).astype(BF16), wkv_ref[...])
    k_pe = _rope(k_rope, cos, sin).astype(BF16)
    for p in range(N_PAIRS):
        lo = p * PAIR_W
        qm_ref[p, :, 0:LANES] = q[:, lo:lo + LANES].astype(BF16)
        qm_ref[p, :, LANES:PAIR_W] = _rope(q[:, lo + LANES:lo + PAIR_W], cos, sin).astype(BF16)
        km_ref[p, :, 0:LANES] = kv[:, p * LANES:(p + 1) * LANES].astype(BF16)
        km_ref[p, :, LANES:PAIR_W] = k_pe
        vm_ref[p] = kv[:, MLA_WIDTH + p * LANES:MLA_WIDTH + (p + 1) * LANES].astype(BF16)


def _in_proj_call(x, mod, g, w_in, gq, wq, gkv, wkv, cos_t, sin_t, seq):
    t = x.shape[0]
    tm = ROW_TILE
    tiles_per_seq = seq // tm
    row = lambda i: (i, 0)
    pair_widths = (LANES, LANES, LANES, PAIR_W, PAIR_W, LANES)
    pair_shapes = tuple(jax.ShapeDtypeStruct((N_PAIRS, t, w), BF16) for w in pair_widths)
    pair_specs = tuple(pl.BlockSpec((N_PAIRS, tm, w), lambda i: (0, i, 0)) for w in pair_widths)
    gate_shape = jax.ShapeDtypeStruct((t, D_MODEL), BF16)
    gate_spec = pl.BlockSpec((tm, D_MODEL), row)
    return pl.pallas_call(
        _in_proj_kernel,
        out_shape=pair_shapes + (gate_shape, gate_shape),
        grid=(t // tm,),
        in_specs=[
            pl.BlockSpec((tm, D_MODEL), row),
            pl.BlockSpec((1, 1, N_MOD * D_MODEL), lambda i: (i // tiles_per_seq, 0, 0)),
            _resident((1, D_MODEL)),
            _resident((D_MODEL, IN_W)),
            _resident((1, MLA_Q_RANK)),
            _resident((MLA_Q_RANK, MLA_QK_W)),
            _resident((1, MLA_KV_RANK)),
            _resident((MLA_KV_RANK, 2 * MLA_WIDTH)),
            pl.BlockSpec((tm, LANES), row),
            pl.BlockSpec((tm, LANES), row),
        ],
        out_specs=pair_specs + (gate_spec, gate_spec),
        compiler_params=pltpu.CompilerParams(
            dimension_semantics=("parallel",), vmem_limit_bytes=VMEM_LIMIT),
        name="in_proj",
    )(x, mod, g, w_in, gq, wq, gkv, wkv, cos_t, sin_t)


def _causal_blocks(nq, depth):
    blocks = [(i, i) for i in range(nq)]
    blocks += [(i, j) for i in range(1, nq) for j in range(i - 1, -1, -1)]
    pad = depth - 1
    table = [blocks[0]] * pad + blocks + [blocks[-1]] * pad
    return (jnp.asarray([e[0] for e in table], jnp.int32),
            jnp.asarray([e[1] for e in table], jnp.int32), len(blocks))


def _block_offset(i):
    return pl.multiple_of(i * ATTN_TILE, ATTN_TILE)


def _run_pipeline(stages, it_ref, jt_ref, nq, n_blocks):
    depth = len(stages)

    def step(diag):
        def body(t, carry):
            for p in range(N_PAIRS):
                for k in range(depth - 1, -1, -1):
                    e = t - k + depth - 1
                    stages[k](p, it_ref[e], jt_ref[e], diag)
            return carry
        return body

    lax.fori_loop(0, nq, step(True), 0)
    lax.fori_loop(nq, n_blocks + depth - 1, step(False), 0)


def _sb_attn_kernel(it_ref, jt_ref, q_ref, k_ref, v_ref, o_ref, later_ref, *scratch,
                    nq, n_blocks):
    blk = ATTN_TILE
    per_pair = [scratch[7 * p:7 * p + 7] for p in range(N_PAIRS)]
    acc_refs, r_refs, ls_refs, hl_refs, rs0_refs, rs1_refs, w_refs = zip(*per_pair)
    lane = lax.broadcasted_iota(jnp.int32, (1, LANES), 1)
    first = lane < HEAD_DIM
    row = lax.broadcasted_iota(jnp.int32, (blk, blk), 0)
    col = lax.broadcasted_iota(jnp.int32, (blk, blk), 1)
    row2 = lax.broadcasted_iota(jnp.int32, (2 * blk, blk), 0) & (blk - 1)
    col2 = lax.broadcasted_iota(jnp.int32, (2 * blk, blk), 1)
    strict = col2 < row2
    later = jnp.where(row > col, 1.0, 0.0).astype(BF16)
    later_ref[0:blk, :] = later
    later_ref[blk:2 * blk, :] = later
    for p in range(N_PAIRS):
        acc_refs[p][...] = jnp.zeros_like(acc_refs[p])
        r_refs[p][...] = jnp.zeros_like(r_refs[p])
        ls_refs[p][...] = jnp.full_like(ls_refs[p], MASKED)
        hl_refs[p][...] = jnp.zeros_like(hl_refs[p])
        rs0_refs[p][...] = jnp.zeros_like(rs0_refs[p])
        rs1_refs[p][...] = jnp.zeros_like(rs1_refs[p])
        w_refs[p][...] = jnp.zeros_like(w_refs[p])

    def scores(p, i, j, diag):
        q = q_ref[p, pl.ds(_block_offset(i), blk), :]
        zero = jnp.zeros_like(q)
        q2 = jnp.concatenate([jnp.where(first, q, zero), jnp.where(first, zero, q)], axis=0)
        z = _dot_nt(q2, k_ref[p, pl.ds(_block_offset(j), blk), :])
        neg_part = jnp.minimum(z, 0.0)
        neg_relu = neg_part - z
        softplus = jnp.log2(1.0 + jnp.exp2(neg_part + neg_relu))
        ls = neg_part - softplus
        lf = neg_relu - softplus
        if diag:
            ls = jnp.where(strict, ls, MASKED)
            lf = jnp.where(strict, lf, 0.0)
        hi = lf.astype(BF16)
        lo = (lf - hi.astype(F32)).astype(BF16)
        rs = jnp.sum(lf, axis=1, keepdims=True)
        ls_refs[p][...] = ls
        hl_refs[p][:, 0:blk] = hi
        hl_refs[p][:, blk:2 * blk] = lo
        rs0_refs[p][...] = jnp.where(first, rs[0:blk], rs[blk:2 * blk])

    def weights(p, i, j, diag):
        cum = _dot(hl_refs[p][...], later_ref[...])
        w_refs[p][...] = jnp.exp2(ls_refs[p][...] + cum).astype(BF16)
        rs1_refs[p][...] = rs0_refs[p][...]

    def accumulate(p, i, j, diag):
        pv = _dot(w_refs[p][...], v_ref[p, pl.ds(_block_offset(j), blk), :])
        pv = jnp.where(first, pv[0:blk], pv[blk:2 * blk])
        r = r_refs[p][i]
        acc_refs[p][i] = acc_refs[p][i] + jnp.exp2(r) * pv
        r_refs[p][i] = r + rs1_refs[p][...]

    _run_pipeline((scores, weights, accumulate), it_ref, jt_ref, nq, n_blocks)
    for p in range(N_PAIRS):
        for i in range(nq):
            o_ref[p, i * blk:(i + 1) * blk, :] = acc_refs[p][i].astype(o_ref.dtype)


def _sb_attn_call(q, k, v):
    n_pairs, b, s, _ = q.shape
    blk = ATTN_TILE
    nq = s // blk
    it, jt, n_blocks = _causal_blocks(nq, 3)
    spec = pl.BlockSpec((n_pairs, None, s, LANES), lambda bi, *_: (0, bi, 0, 0))
    grid_spec = pltpu.PrefetchScalarGridSpec(
        num_scalar_prefetch=2,
        grid=(b,),
        in_specs=[spec, spec, spec],
        out_specs=spec,
        scratch_shapes=[pltpu.VMEM((2 * blk, blk), BF16)] + n_pairs * [
            pltpu.VMEM((nq, blk, LANES), F32),
            pltpu.VMEM((nq, blk, LANES), F32),
            pltpu.VMEM((2 * blk, blk), F32),
            pltpu.VMEM((2 * blk, 2 * blk), BF16),
            pltpu.VMEM((blk, LANES), F32),
            pltpu.VMEM((blk, LANES), F32),
            pltpu.VMEM((2 * blk, blk), BF16),
        ],
    )
    return pl.pallas_call(
        functools.partial(_sb_attn_kernel, nq=nq, n_blocks=n_blocks),
        out_shape=jax.ShapeDtypeStruct(q.shape, BF16),
        grid_spec=grid_spec,
        compiler_params=pltpu.CompilerParams(
            dimension_semantics=("parallel",), vmem_limit_bytes=VMEM_LIMIT),
        name="sb_attn",
    )(it, jt, q, k, v)


def _mla_attn_kernel(it_ref, jt_ref, q_ref, k_ref, v_ref, o_ref, vext_ref, *scratch,
                     nq, n_blocks):
    blk = ATTN_TILE
    per_pair = [scratch[5 * p:5 * p + 5] for p in range(N_PAIRS)]
    acc_refs, l_refs, m_refs, alpha_refs, p_refs = zip(*per_pair)
    lane = lax.broadcasted_iota(jnp.int32, (1, LANES), 1)
    first = lane < HEAD_DIM
    lane2 = lax.broadcasted_iota(jnp.int32, (1, PAIR_W), 1)
    own_a = (lane2 < HEAD_DIM) | ((lane2 >= LANES) & (lane2 < LANES + MLA_ROPE_DIM))
    own_b = ((lane2 >= HEAD_DIM) & (lane2 < LANES)) | (
        (lane2 >= LANES + MLA_ROPE_DIM) & (lane2 < LANES + 2 * MLA_ROPE_DIM))
    row2 = lax.broadcasted_iota(jnp.int32, (2 * blk, blk), 0) & (blk - 1)
    col2 = lax.broadcasted_iota(jnp.int32, (2 * blk, blk), 1)
    causal = col2 <= row2
    ones = jnp.ones((v_ref.shape[1], LANES), BF16)
    for p in range(N_PAIRS):
        vext_ref[p, :, 0:LANES] = v_ref[p]
        vext_ref[p, :, LANES:PAIR_W] = ones
        acc_refs[p][...] = jnp.zeros_like(acc_refs[p])
        l_refs[p][...] = jnp.zeros_like(l_refs[p])
        m_refs[p][...] = jnp.full_like(m_refs[p], MASKED)
        alpha_refs[p][...] = jnp.zeros_like(alpha_refs[p])
        p_refs[p][...] = jnp.zeros_like(p_refs[p])

    def probabilities(p, i, j, diag):
        q = q_ref[p, pl.ds(_block_offset(i), blk), :]
        zero = jnp.zeros_like(q)
        q2 = jnp.concatenate([jnp.where(own_a, q, zero), jnp.where(own_b, q, zero)], axis=0)
        sc = _dot_nt(q2, k_ref[p, pl.ds(_block_offset(j), blk), :])
        if diag:
            sc = jnp.where(causal, sc, MASKED)
        m_old = m_refs[p][i]
        m_new = jnp.maximum(m_old, jnp.max(sc, axis=1, keepdims=True))
        alpha_refs[p][...] = jnp.exp2(m_old - m_new)
        m_refs[p][i] = m_new
        p_refs[p][...] = jnp.exp2(sc - jnp.concatenate([m_new, m_new], axis=1)).astype(BF16)

    def accumulate(p, i, j, diag):
        pv = _dot(p_refs[p][...], vext_ref[p, pl.ds(_block_offset(j), blk), :])
        alpha = alpha_refs[p][...]
        alpha = jnp.where(first, alpha[0:blk], alpha[blk:2 * blk])
        acc_refs[p][i] = acc_refs[p][i] * alpha + jnp.where(
            first, pv[0:blk, 0:LANES], pv[blk:2 * blk, 0:LANES])
        l_refs[p][i] = l_refs[p][i] * alpha + jnp.where(
            first, pv[0:blk, LANES:PAIR_W], pv[blk:2 * blk, LANES:PAIR_W])

    _run_pipeline((probabilities, accumulate), it_ref, jt_ref, nq, n_blocks)
    for p in range(N_PAIRS):
        for i in range(nq):
            o_ref[p, i * blk:(i + 1) * blk, :] = (acc_refs[p][i] / l_refs[p][i]).astype(o_ref.dtype)


def _mla_attn_call(q, k, v):
    n_pairs, b, s, _ = v.shape
    blk = ATTN_TILE
    nq = s // blk
    it, jt, n_blocks = _causal_blocks(nq, 2)
    qk_spec = pl.BlockSpec((n_pairs, None, s, PAIR_W), lambda bi, *_: (0, bi, 0, 0))
    v_spec = pl.BlockSpec((n_pairs, None, s, LANES), lambda bi, *_: (0, bi, 0, 0))
    grid_spec = pltpu.PrefetchScalarGridSpec(
        num_scalar_prefetch=2,
        grid=(b,),
        in_specs=[qk_spec, qk_spec, v_spec],
        out_specs=v_spec,
        scratch_shapes=[pltpu.VMEM((n_pairs, s, PAIR_W), BF16)] + n_pairs * [
            pltpu.VMEM((nq, blk, LANES), F32),
            pltpu.VMEM((nq, blk, LANES), F32),
            pltpu.VMEM((nq, 2 * blk, LANES), F32),
            pltpu.VMEM((2 * blk, LANES), F32),
            pltpu.VMEM((2 * blk, blk), BF16),
        ],
    )
    return pl.pallas_call(
        functools.partial(_mla_attn_kernel, nq=nq, n_blocks=n_blocks),
        out_shape=jax.ShapeDtypeStruct(v.shape, BF16),
        grid_spec=grid_spec,
        compiler_params=pltpu.CompilerParams(
            dimension_semantics=("parallel",), vmem_limit_bytes=VMEM_LIMIT),
        name="mla_attn",
    )(it, jt, q, k, v)


def _post_kernel(x_ref, osb_ref, omla_ref, sgs_ref, sgm_ref, mod_ref,
                 wsb_ref, wmla_ref, wmix_ref, g2_ref, wup_ref, wdn_ref, gf_ref,
                 o_ref, *, final):
    mod = mod_ref[0]
    gate1 = mod[:, 2 * D_MODEL:3 * D_MODEL]
    shift2 = mod[:, 3 * D_MODEL:4 * D_MODEL]
    scale2 = mod[:, 4 * D_MODEL:5 * D_MODEL]
    gate2 = mod[:, 5 * D_MODEL:6 * D_MODEL]

    o_sb = _dot(jnp.concatenate([osb_ref[p] for p in range(N_PAIRS)], axis=1), wsb_ref[...])
    o_mla = _dot(jnp.concatenate([omla_ref[p] for p in range(N_PAIRS)], axis=1), wmla_ref[...])
    merged = sgs_ref[...].astype(F32) * o_sb + sgm_ref[...].astype(F32) * o_mla
    x = x_ref[...] + gate1 * _dot(merged.astype(BF16), wmix_ref[...])

    h = _rms(x, g2_ref[...]) * (1.0 + scale2) + shift2
    u = jnp.maximum(_dot(h.astype(BF16), wup_ref[...]), 0.0)
    x = x + gate2 * _dot((u * u).astype(BF16), wdn_ref[...])
    if final:
        x = _rms(x, gf_ref[...])
    o_ref[...] = x


def _post_call(x, o_sb, o_mla, sg_sb, sg_mla, mod, w_sb, w_mla, w_mix, g2, w_up, w_dn, g_final,
               seq, final):
    t = x.shape[0]
    tm = ROW_TILE
    tiles_per_seq = seq // tm
    row = lambda i: (i, 0)
    pair_spec = pl.BlockSpec((N_PAIRS, tm, LANES), lambda i: (0, i, 0))
    return pl.pallas_call(
        functools.partial(_post_kernel, final=final),
        out_shape=jax.ShapeDtypeStruct((t, D_MODEL), F32),
        grid=(t // tm,),
        in_specs=[
            pl.BlockSpec((tm, D_MODEL), row),
            pair_spec,
            pair_spec,
            pl.BlockSpec((tm, D_MODEL), row),
            pl.BlockSpec((tm, D_MODEL), row),
            pl.BlockSpec((1, 1, N_MOD * D_MODEL), lambda i: (i // tiles_per_seq, 0, 0)),
            _resident((SB_WIDTH, D_MODEL)),
            _resident((MLA_WIDTH, D_MODEL)),
            _resident((D_MODEL, D_MODEL)),
            _resident((1, D_MODEL)),
            _resident((D_MODEL, D_FF)),
            _resident((D_FF, D_MODEL)),
            _resident((1, D_MODEL)),
        ],
        out_specs=pl.BlockSpec((tm, D_MODEL), row),
        compiler_params=pltpu.CompilerParams(
            dimension_semantics=("parallel",), vmem_limit_bytes=VMEM_LIMIT),
        name="post_final" if final else "post",
    )(x, o_sb, o_mla, sg_sb, sg_mla, mod, w_sb, w_mla, w_mix, g2, w_up, w_dn, g_final)


def _layout_w_in(w_in):
    depth, d, _ = w_in.shape
    head = w_in[..., 0:OFF_KROPE]
    k_rope = w_in[..., OFF_KROPE:OFF_KROPE + MLA_ROPE_DIM]
    pad = jnp.zeros((depth, d, LANES - 2 * MLA_ROPE_DIM), w_in.dtype)
    gates = w_in[..., OFF_KROPE + MLA_ROPE_DIM:]
    return jnp.concatenate([head, k_rope, k_rope, pad, gates], axis=-1).astype(BF16)


def _layout_w_q_up(w_q_up):
    depth, r, _ = w_q_up.shape
    w = w_q_up.reshape(depth, r, N_PAIRS, 2, MLA_QK_DIM)
    nope = w[..., :HEAD_DIM].reshape(depth, r, N_PAIRS, LANES)
    rope = w[..., HEAD_DIM:].reshape(depth, r, N_PAIRS, 2 * MLA_ROPE_DIM)
    pad = jnp.zeros((depth, r, N_PAIRS, LANES - 2 * MLA_ROPE_DIM), w.dtype)
    return jnp.concatenate([nope, rope, pad], axis=-1).reshape(depth, r, MLA_QK_W).astype(BF16)


def _layout_w_kv_up(w_kv_up):
    depth, r, _ = w_kv_up.shape
    w = w_kv_up.reshape(depth, r, MLA_HEADS, 2 * HEAD_DIM)
    k = w[..., :HEAD_DIM].reshape(depth, r, MLA_WIDTH)
    v = w[..., HEAD_DIM:].reshape(depth, r, MLA_WIDTH)
    return jnp.concatenate([k, v], axis=-1).astype(BF16)


def kernel(x, c, positions, w_ada, b_ada, g_mix_norm, w_in, g_q_lat, w_q_up, g_kv_lat, w_kv_up,
           w_sb_out, w_mla_out, w_mix_out, g_mlp_norm, w_up, w_down, g_final):
    batch, seq, d = x.shape
    depth = w_in.shape[0]
    t = batch * seq

    w_in_l = _layout_w_in(w_in)
    w_q_l = _layout_w_q_up(w_q_up)
    w_kv_l = _layout_w_kv_up(w_kv_up)
    w_sb_b, w_mla_b, w_mix_b = (w.astype(BF16) for w in (w_sb_out, w_mla_out, w_mix_out))
    w_up_b, w_dn_b = w_up.astype(BF16), w_down.astype(BF16)

    mod_all = _mod_call(c, w_ada, b_ada).reshape(depth, batch, 1, N_MOD * d)
    cos_t, sin_t = _rope_tables(positions)

    def per_batch(a):
        return a.reshape(N_PAIRS, batch, seq, a.shape[-1])

    xt = x.reshape(t, d)
    for l in range(depth):
        mod = mod_all[l]
        q_sb, k_sb, v_sb, q_m, k_m, v_m, sg_sb, sg_mla = _in_proj_call(
            xt, mod, g_mix_norm[l].reshape(1, d), w_in_l[l],
            g_q_lat[l].reshape(1, -1), w_q_l[l], g_kv_lat[l].reshape(1, -1), w_kv_l[l],
            cos_t, sin_t, seq)
        o_sb = _sb_attn_call(per_batch(q_sb), per_batch(k_sb), per_batch(v_sb))
        o_mla = _mla_attn_call(per_batch(q_m), per_batch(k_m), per_batch(v_m))
        xt = _post_call(xt, o_sb.reshape(N_PAIRS, t, LANES), o_mla.reshape(N_PAIRS, t, LANES),
                        sg_sb, sg_mla, mod,
                        w_sb_b[l], w_mla_b[l], w_mix_b[l], g_mlp_norm[l].reshape(1, d),
                        w_up_b[l], w_dn_b[l], g_final.reshape(1, d), seq, l == depth - 1)
    return xt.reshape(batch, seq, d)
```

```python
import functools
import math

import jax
import jax.numpy as jnp
from jax import lax
from jax.experimental import pallas as pl
from jax.experimental.pallas import tpu as pltpu

D_MODEL = 1024
SB_HEADS = 8
HEAD_DIM = 64
SB_WIDTH = SB_HEADS * HEAD_DIM
MLA_HEADS = 8
MLA_ROPE_DIM = 32
MLA_QK_DIM = HEAD_DIM + MLA_ROPE_DIM
MLA_Q_RANK = 384
MLA_KV_RANK = 256
MLA_WIDTH = MLA_HEADS * HEAD_DIM
D_FF = 4 * D_MODEL
ROPE_THETA = 10000.0
NORM_EPS = 1e-6
N_MOD = 6
LOG2E = math.log2(math.e)

LANES = 128
PAIR_W = 2 * LANES
N_PAIRS = SB_HEADS // 2
MLA_QK_W = N_PAIRS * PAIR_W

OFF_QKV = 0
OFF_QLAT = 3 * SB_WIDTH
OFF_KVLAT = OFF_QLAT + MLA_Q_RANK
OFF_KROPE = OFF_KVLAT + MLA_KV_RANK
OFF_GATES = OFF_KROPE + LANES
IN_W = OFF_GATES + 2 * D_MODEL

ROW_TILE = 512
ATTN_TILE = 256
VMEM_LIMIT = 56 * 1024 * 1024
MASKED = -1e30

F32 = jnp.float32
BF16 = jnp.bfloat16


def _resident(shape, layer=None):
    nd = len(shape)
    if layer is None:
        return pl.BlockSpec(shape, lambda *_: (0,) * nd, pipeline_mode=pl.Buffered(1))
    return pl.BlockSpec((None,) + tuple(shape), lambda *_: (layer,) + (0,) * nd,
                        pipeline_mode=pl.Buffered(1))


def _rms(x, g):
    return x * lax.rsqrt(jnp.mean(x * x, axis=-1, keepdims=True) + NORM_EPS) * g


def _dot(a, b):
    return jnp.dot(a, b, preferred_element_type=F32)


def _dot_nt(a, b):
    return lax.dot_general(a, b, (((1,), (1,)), ((), ())), preferred_element_type=F32)


def _mod_kernel(c_ref, w_ref, b_ref, o_ref):
    c = c_ref[...]
    c_act = c * (1.0 / (1.0 + jnp.exp(-c)))
    o_ref[0] = _dot(c_act.astype(BF16), w_ref[0].astype(BF16)) + b_ref[0]


def _mod_call(c, w_ada, b_ada):
    depth, d, n = w_ada.shape
    batch = c.shape[0]
    tn = 1536
    return pl.pallas_call(
        _mod_kernel,
        out_shape=jax.ShapeDtypeStruct((depth, batch, n), F32),
        grid=(depth, n // tn),
        in_specs=[
            pl.BlockSpec((batch, d), lambda l, j: (0, 0)),
            pl.BlockSpec((1, d, tn), lambda l, j: (l, 0, j)),
            pl.BlockSpec((1, 1, tn), lambda l, j: (l, 0, j)),
        ],
        out_specs=pl.BlockSpec((1, batch, tn), lambda l, j: (l, 0, j)),
        compiler_params=pltpu.CompilerParams(
            dimension_semantics=("parallel", "parallel"), vmem_limit_bytes=VMEM_LIMIT),
        name="adaln_mod",
    )(c, w_ada, b_ada.reshape(depth, 1, n))


def _rope_table_kernel(pos_ref, inv_ref, sign_ref, cos_ref, sin_ref):
    ang = pos_ref[...].astype(F32) * inv_ref[...]
    cos_ref[...] = jnp.cos(ang)
    sin_ref[...] = jnp.sin(ang) * sign_ref[...]


def _rope_tables(positions):
    t = positions.size
    half = MLA_ROPE_DIM // 2
    inv_freq = 1.0 / (ROPE_THETA ** (jnp.arange(0, MLA_ROPE_DIM, 2, dtype=F32) / MLA_ROPE_DIM))
    lane = jnp.arange(LANES)
    inv_lane = inv_freq[lane % half].reshape(1, LANES)
    sign_lane = jnp.where((lane % MLA_ROPE_DIM) < half, -1.0, 1.0).astype(F32).reshape(1, LANES)
    ts = 2048
    return pl.pallas_call(
        _rope_table_kernel,
        out_shape=(jax.ShapeDtypeStruct((t, LANES), F32),) * 2,
        grid=(t // ts,),
        in_specs=[
            pl.BlockSpec((ts, 1), lambda i: (i, 0)),
            pl.BlockSpec((1, LANES), lambda i: (0, 0)),
            pl.BlockSpec((1, LANES), lambda i: (0, 0)),
        ],
        out_specs=(pl.BlockSpec((ts, LANES), lambda i: (i, 0)),) * 2,
        compiler_params=pltpu.CompilerParams(dimension_semantics=("parallel",)),
        name="rope_tables",
    )(positions.reshape(t, 1), inv_lane, sign_lane)


def _rope(t, cos, sin_signed):
    lane = lax.broadcasted_iota(jnp.int32, (1, LANES), 1)
    low = (lane % MLA_ROPE_DIM) < (MLA_ROPE_DIM // 2)
    half = MLA_ROPE_DIM // 2
    swapped = jnp.where(low, pltpu.roll(t, LANES - half, 1), pltpu.roll(t, half, 1))
    return t * cos + swapped * sin_signed


def _in_proj_kernel(x_ref, mod_ref, g_ref, w_ref, gq_ref, wq_ref, gkv_ref, wkv_ref,
                    cos_ref, sin_ref,
                    qsb_ref, ksb_ref, vsb_ref, qm_ref, km_ref, vm_ref, sgs_ref, sgm_ref):
    mod = mod_ref[0]
    shift1 = mod[:, 0:D_MODEL]
    scale1 = mod[:, D_MODEL:2 * D_MODEL]
    h = _rms(x_ref[...], g_ref[...]) * (1.0 + scale1) + shift1
    hb = h.astype(BF16)

    lat = _dot(hb, w_ref[:, OFF_QLAT:OFF_GATES])
    q_lat = lat[:, 0:MLA_Q_RANK]
    kv_lat = lat[:, MLA_Q_RANK:MLA_Q_RANK + MLA_KV_RANK]
    k_rope = lat[:, MLA_Q_RANK + MLA_KV_RANK:]

    qkv = _dot(hb, w_ref[:, OFF_QKV:OFF_QLAT])
    for p in range(N_PAIRS):
        lo = p * LANES
        qsb_ref[p] = (qkv[:, lo:lo + LANES] * (HEAD_DIM ** -0.5 * LOG2E)).astype(BF16)
        ksb_ref[p] = qkv[:, SB_WIDTH + lo:SB_WIDTH + lo + LANES].astype(BF16)
        vsb_ref[p] = qkv[:, 2 * SB_WIDTH + lo:2 * SB_WIDTH + lo + LANES].astype(BF16)

    gates = _dot(hb, w_ref[:, OFF_GATES:IN_W])
    sig = 1.0 / (1.0 + jnp.exp(-gates))
    sgs_ref[...] = sig[:, 0:D_MODEL].astype(BF16)
    sgm_ref[...] = sig[:, D_MODEL:2 * D_MODEL].astype(BF16)

    cos = cos_ref[...]
    sin = sin_ref[...]

    q = _dot(_rms(q_lat, gq_ref[...]).astype(BF16), wq_ref[...]) * (MLA_QK_DIM ** -0.5 * LOG2E)
    kv = _dot(_rms(kv_lat, gkv_ref[...]).astype(BF16), wkv_ref[...])
    k_pe = _rope(k_rope, cos, sin).astype(BF16)
    for p in range(N_PAIRS):
        lo = p * PAIR_W
        qm_ref[p, :, 0:LANES] = q[:, lo:lo + LANES].astype(BF16)
        qm_ref[p, :, LANES:PAIR_W] = _rope(q[:, lo + LANES:lo + PAIR_W], cos, sin).astype(BF16)
        km_ref[p, :, 0:LANES] = kv[:, p * LANES:(p + 1) * LANES].astype(BF16)
        km_ref[p, :, LANES:PAIR_W] = k_pe
        vm_ref[p] = kv[:, MLA_WIDTH + p * LANES:MLA_WIDTH + (p + 1) * LANES].astype(BF16)


def _in_proj_call(x, mod, g, w_in, gq, wq, gkv, wkv, cos_t, sin_t, seq, layer):
    t = x.shape[0]
    tm = ROW_TILE
    tiles_per_seq = seq // tm
    row = lambda i: (i, 0)
    pair_widths = (LANES, LANES, LANES, PAIR_W, PAIR_W, LANES)
    pair_shapes = tuple(jax.ShapeDtypeStruct((N_PAIRS, t, w), BF16) for w in pair_widths)
    pair_specs = tuple(pl.BlockSpec((N_PAIRS, tm, w), lambda i: (0, i, 0)) for w in pair_widths)
    gate_shape = jax.ShapeDtypeStruct((t, D_MODEL), BF16)
    gate_spec = pl.BlockSpec((tm, D_MODEL), row)
    return pl.pallas_call(
        _in_proj_kernel,
        out_shape=pair_shapes + (gate_shape, gate_shape),
        grid=(t // tm,),
        in_specs=[
            pl.BlockSpec((tm, D_MODEL), row),
            pl.BlockSpec((1, 1, N_MOD * D_MODEL), lambda i: (i // tiles_per_seq, 0, 0)),
            _resident((1, D_MODEL)),
            _resident((D_MODEL, IN_W), layer),
            _resident((1, MLA_Q_RANK)),
            _resident((MLA_Q_RANK, MLA_QK_W), layer),
            _resident((1, MLA_KV_RANK)),
            _resident((MLA_KV_RANK, 2 * MLA_WIDTH), layer),
            pl.BlockSpec((tm, LANES), row),
            pl.BlockSpec((tm, LANES), row),
        ],
        out_specs=pair_specs + (gate_spec, gate_spec),
        compiler_params=pltpu.CompilerParams(
            dimension_semantics=("parallel",), vmem_limit_bytes=VMEM_LIMIT),
        name="in_proj",
    )(x, mod, g, w_in, gq, wq, gkv, wkv, cos_t, sin_t)


def _causal_blocks(nq, depth):
    blocks = [(i, i) for i in range(nq)]
    blocks += [(i, j) for i in range(1, nq) for j in range(i - 1, -1, -1)]
    pad = depth - 1
    table = [blocks[0]] * pad + blocks + [blocks[-1]] * pad
    return (jnp.asarray([e[0] for e in table], jnp.int32),
            jnp.asarray([e[1] for e in table], jnp.int32), len(blocks))


def _block_offset(i):
    return pl.multiple_of(i * ATTN_TILE, ATTN_TILE)


def _run_pipeline(stages, it_ref, jt_ref, nq, n_blocks):
    depth = len(stages)

    def step(diag):
        def body(t, carry):
            for p in range(N_PAIRS):
                for k in range(depth - 1, -1, -1):
                    e = t - k + depth - 1
                    stages[k](p, it_ref[e], jt_ref[e], diag)
            return carry
        return body

    lax.fori_loop(0, nq, step(True), 0)
    lax.fori_loop(nq, n_blocks + depth - 1, step(False), 0)


def _sb_attn_kernel(it_ref, jt_ref, q_ref, k_ref, v_ref, o_ref, later_ref, *scratch,
                    nq, n_blocks):
    blk = ATTN_TILE
    per_pair = [scratch[7 * p:7 * p + 7] for p in range(N_PAIRS)]
    acc_refs, r_refs, ls_refs, lf_refs, rs0_refs, rs1_refs, w_refs = zip(*per_pair)
    lane = lax.broadcasted_iota(jnp.int32, (1, LANES), 1)
    first = lane < HEAD_DIM
    row = lax.broadcasted_iota(jnp.int32, (blk, blk), 0)
    col = lax.broadcasted_iota(jnp.int32, (blk, blk), 1)
    row2 = lax.broadcasted_iota(jnp.int32, (2 * blk, blk), 0) & (blk - 1)
    col2 = lax.broadcasted_iota(jnp.int32, (2 * blk, blk), 1)
    strict = col2 < row2
    later_ref[...] = jnp.where(row > col, 1.0, 0.0).astype(BF16)
    for p in range(N_PAIRS):
        acc_refs[p][...] = jnp.zeros_like(acc_refs[p])
        r_refs[p][...] = jnp.zeros_like(r_refs[p])
        ls_refs[p][...] = jnp.full_like(ls_refs[p], MASKED)
        lf_refs[p][...] = jnp.zeros_like(lf_refs[p])
        rs0_refs[p][...] = jnp.zeros_like(rs0_refs[p])
        rs1_refs[p][...] = jnp.zeros_like(rs1_refs[p])
        w_refs[p][...] = jnp.zeros_like(w_refs[p])

    def scores(p, i, j, diag):
        q = q_ref[p, pl.ds(_block_offset(i), blk), :]
        zero = jnp.zeros_like(q)
        q2 = jnp.concatenate([jnp.where(first, q, zero), jnp.where(first, zero, q)], axis=0)
        z = _dot_nt(q2, k_ref[p, pl.ds(_block_offset(j), blk), :])
        neg_part = jnp.minimum(z, 0.0)
        neg_relu = neg_part - z
        softplus = jnp.log2(1.0 + jnp.exp2(neg_part + neg_relu))
        ls = neg_part - softplus
        lf = neg_relu - softplus
        if diag:
            ls = jnp.where(strict, ls, MASKED)
            lf = jnp.where(strict, lf, 0.0)
        rs = jnp.sum(lf, axis=1, keepdims=True)
        ls_refs[p][...] = ls
        lf_refs[p][...] = lf.astype(BF16)
        rs0_refs[p][...] = jnp.where(first, rs[0:blk], rs[blk:2 * blk])

    def weights(p, i, j, diag):
        cum = _dot(lf_refs[p][...], later_ref[...])
        w_refs[p][...] = jnp.exp2(ls_refs[p][...] + cum).astype(BF16)
        rs1_refs[p][...] = rs0_refs[p][...]

    def accumulate(p, i, j, diag):
        pv = _dot(w_refs[p][...], v_ref[p, pl.ds(_block_offset(j), blk), :])
        pv = jnp.where(first, pv[0:blk], pv[blk:2 * blk])
        r = r_refs[p][i]
        acc_refs[p][i] = acc_refs[p][i] + jnp.exp2(r) * pv
        r_refs[p][i] = r + rs1_refs[p][...]

    _run_pipeline((scores, weights, accumulate), it_ref, jt_ref, nq, n_blocks)
    for p in range(N_PAIRS):
        for i in range(nq):
            o_ref[p, i * blk:(i + 1) * blk, :] = acc_refs[p][i].astype(o_ref.dtype)


def _sb_attn_call(q, k, v):
    n_pairs, b, s, _ = q.shape
    blk = ATTN_TILE
    nq = s // blk
    it, jt, n_blocks = _causal_blocks(nq, 3)
    spec = pl.BlockSpec((n_pairs, None, s, LANES), lambda bi, *_: (0, bi, 0, 0))
    grid_spec = pltpu.PrefetchScalarGridSpec(
        num_scalar_prefetch=2,
        grid=(b,),
        in_specs=[spec, spec, spec],
        out_specs=spec,
        scratch_shapes=[pltpu.VMEM((blk, blk), BF16)] + n_pairs * [
            pltpu.VMEM((nq, blk, LANES), F32),
            pltpu.VMEM((nq, blk, LANES), F32),
            pltpu.VMEM((2 * blk, blk), F32),
            pltpu.VMEM((2 * blk, blk), BF16),
            pltpu.VMEM((blk, LANES), F32),
            pltpu.VMEM((blk, LANES), F32),
            pltpu.VMEM((2 * blk, blk), BF16),
        ],
    )
    return pl.pallas_call(
        functools.partial(_sb_attn_kernel, nq=nq, n_blocks=n_blocks),
        out_shape=jax.ShapeDtypeStruct(q.shape, BF16),
        grid_spec=grid_spec,
        compiler_params=pltpu.CompilerParams(
            dimension_semantics=("parallel",), vmem_limit_bytes=VMEM_LIMIT),
        name="sb_attn",
    )(it, jt, q, k, v)


def _mla_attn_kernel(it_ref, jt_ref, q_ref, k_ref, v_ref, o_ref, vext_ref, *scratch,
                     nq, n_blocks):
    blk = ATTN_TILE
    per_pair = [scratch[5 * p:5 * p + 5] for p in range(N_PAIRS)]
    acc_refs, l_refs, m_refs, alpha_refs, p_refs = zip(*per_pair)
    lane = lax.broadcasted_iota(jnp.int32, (1, LANES), 1)
    first = lane < HEAD_DIM
    lane2 = lax.broadcasted_iota(jnp.int32, (1, PAIR_W), 1)
    own_a = (lane2 < HEAD_DIM) | ((lane2 >= LANES) & (lane2 < LANES + MLA_ROPE_DIM))
    own_b = ((lane2 >= HEAD_DIM) & (lane2 < LANES)) | (
        (lane2 >= LANES + MLA_ROPE_DIM) & (lane2 < LANES + 2 * MLA_ROPE_DIM))
    row2 = lax.broadcasted_iota(jnp.int32, (2 * blk, blk), 0) & (blk - 1)
    col2 = lax.broadcasted_iota(jnp.int32, (2 * blk, blk), 1)
    causal = col2 <= row2
    ones = jnp.ones((v_ref.shape[1], LANES), BF16)
    for p in range(N_PAIRS):
        vext_ref[p, :, 0:LANES] = v_ref[p]
        vext_ref[p, :, LANES:PAIR_W] = ones
        acc_refs[p][...] = jnp.zeros_like(acc_refs[p])
        l_refs[p][...] = jnp.zeros_like(l_refs[p])
        m_refs[p][...] = jnp.full_like(m_refs[p], MASKED)
        alpha_refs[p][...] = jnp.zeros_like(alpha_refs[p])
        p_refs[p][...] = jnp.zeros_like(p_refs[p])

    def probabilities(p, i, j, diag):
        q = q_ref[p, pl.ds(_block_offset(i), blk), :]
        zero = jnp.zeros_like(q)
        q2 = jnp.concatenate([jnp.where(own_a, q, zero), jnp.where(own_b, q, zero)], axis=0)
        sc = _dot_nt(q2, k_ref[p, pl.ds(_block_offset(j), blk), :])
        if diag:
            sc = jnp.where(causal, sc, MASKED)
        m_old = m_refs[p][i]
        m_new = jnp.maximum(m_old, jnp.max(sc, axis=1, keepdims=True))
        alpha_refs[p][...] = jnp.exp2(m_old - m_new)
        m_refs[p][i] = m_new
        p_refs[p][...] = jnp.exp2(sc - jnp.concatenate([m_new, m_new], axis=1)).astype(BF16)

    def accumulate(p, i, j, diag):
        pv = _dot(p_refs[p][...], vext_ref[p, pl.ds(_block_offset(j), blk), :])
        alpha = alpha_refs[p][...]
        alpha = jnp.where(first, alpha[0:blk], alpha[blk:2 * blk])
        acc_refs[p][i] = acc_refs[p][i] * alpha + jnp.where(
            first, pv[0:blk, 0:LANES], pv[blk:2 * blk, 0:LANES])
        l_refs[p][i] = l_refs[p][i] * alpha + jnp.where(
            first, pv[0:blk, LANES:PAIR_W], pv[blk:2 * blk, LANES:PAIR_W])

    _run_pipeline((probabilities, accumulate), it_ref, jt_ref, nq, n_blocks)
    for p in range(N_PAIRS):
        for i in range(nq):
            o_ref[p, i * blk:(i + 1) * blk, :] = (acc_refs[p][i] / l_refs[p][i]).astype(o_ref.dtype)


def _mla_attn_call(q, k, v):
    n_pairs, b, s, _ = v.shape
    blk = ATTN_TILE
    nq = s // blk
    it, jt, n_blocks = _causal_blocks(nq, 2)
    qk_spec = pl.BlockSpec((n_pairs, None, s, PAIR_W), lambda bi, *_: (0, bi, 0, 0))
    v_spec = pl.BlockSpec((n_pairs, None, s, LANES), lambda bi, *_: (0, bi, 0, 0))
    grid_spec = pltpu.PrefetchScalarGridSpec(
        num_scalar_prefetch=2,
        grid=(b,),
        in_specs=[qk_spec, qk_spec, v_spec],
        out_specs=v_spec,
        scratch_shapes=[pltpu.VMEM((n_pairs, s, PAIR_W), BF16)] + n_pairs * [
            pltpu.VMEM((nq, blk, LANES), F32),
            pltpu.VMEM((nq, blk, LANES), F32),
            pltpu.VMEM((nq, 2 * blk, LANES), F32),
            pltpu.VMEM((2 * blk, LANES), F32),
            pltpu.VMEM((2 * blk, blk), BF16),
        ],
    )
    return pl.pallas_call(
        functools.partial(_mla_attn_kernel, nq=nq, n_blocks=n_blocks),
        out_shape=jax.ShapeDtypeStruct(v.shape, BF16),
        grid_spec=grid_spec,
        compiler_params=pltpu.CompilerParams(
            dimension_semantics=("parallel",), vmem_limit_bytes=VMEM_LIMIT),
        name="mla_attn",
    )(it, jt, q, k, v)


def _post_kernel(x_ref, osb_ref, omla_ref, sgs_ref, sgm_ref, mod_ref,
                 wsb_ref, wmla_ref, wmix_ref, g2_ref, wup_ref, wdn_ref, gf_ref,
                 o_ref, *, final):
    mod = mod_ref[0]
    gate1 = mod[:, 2 * D_MODEL:3 * D_MODEL]
    shift2 = mod[:, 3 * D_MODEL:4 * D_MODEL]
    scale2 = mod[:, 4 * D_MODEL:5 * D_MODEL]
    gate2 = mod[:, 5 * D_MODEL:6 * D_MODEL]

    o_sb = _dot(jnp.concatenate([osb_ref[p] for p in range(N_PAIRS)], axis=1), wsb_ref[...])
    o_mla = _dot(jnp.concatenate([omla_ref[p] for p in range(N_PAIRS)], axis=1), wmla_ref[...])
    merged = sgs_ref[...].astype(F32) * o_sb + sgm_ref[...].astype(F32) * o_mla
    x = x_ref[...] + gate1 * _dot(merged.astype(BF16), wmix_ref[...])

    h = _rms(x, g2_ref[...]) * (1.0 + scale2) + shift2
    u = jnp.maximum(_dot(h.astype(BF16), wup_ref[...]), 0.0)
    x = x + gate2 * _dot((u * u).astype(BF16), wdn_ref[...])
    if final:
        x = _rms(x, gf_ref[...])
    o_ref[...] = x


def _post_call(x, o_sb, o_mla, sg_sb, sg_mla, mod, w_sb, w_mla, w_mix, g2, w_up, w_dn, g_final,
               seq, layer, final):
    t = x.shape[0]
    tm = ROW_TILE
    tiles_per_seq = seq // tm
    row = lambda i: (i, 0)
    pair_spec = pl.BlockSpec((N_PAIRS, tm, LANES), lambda i: (0, i, 0))
    return pl.pallas_call(
        functools.partial(_post_kernel, final=final),
        out_shape=jax.ShapeDtypeStruct((t, D_MODEL), F32),
        grid=(t // tm,),
        in_specs=[
            pl.BlockSpec((tm, D_MODEL), row),
            pair_spec,
            pair_spec,
            pl.BlockSpec((tm, D_MODEL), row),
            pl.BlockSpec((tm, D_MODEL), row),
            pl.BlockSpec((1, 1, N_MOD * D_MODEL), lambda i: (i // tiles_per_seq, 0, 0)),
            _resident((SB_WIDTH, D_MODEL), layer),
            _resident((MLA_WIDTH, D_MODEL), layer),
            _resident((D_MODEL, D_MODEL), layer),
            _resident((1, D_MODEL)),
            _resident((D_MODEL, D_FF), layer),
            _resident((D_FF, D_MODEL), layer),
            _resident((1, D_MODEL)),
        ],
        out_specs=pl.BlockSpec((tm, D_MODEL), row),
        compiler_params=pltpu.CompilerParams(
            dimension_semantics=("parallel",), vmem_limit_bytes=VMEM_LIMIT),
        name="post_final" if final else "post",
    )(x, o_sb, o_mla, sg_sb, sg_mla, mod, w_sb, w_mla, w_mix, g2, w_up, w_dn, g_final)


def _layout_w_in(w_in):
    depth, d, _ = w_in.shape
    head = w_in[..., 0:OFF_KROPE]
    k_rope = w_in[..., OFF_KROPE:OFF_KROPE + MLA_ROPE_DIM]
    pad = jnp.zeros((depth, d, LANES - 2 * MLA_ROPE_DIM), w_in.dtype)
    gates = w_in[..., OFF_KROPE + MLA_ROPE_DIM:]
    return jnp.concatenate([head, k_rope, k_rope, pad, gates], axis=-1).astype(BF16)


def _layout_w_q_up(w_q_up):
    depth, r, _ = w_q_up.shape
    w = w_q_up.reshape(depth, r, N_PAIRS, 2, MLA_QK_DIM)
    nope = w[..., :HEAD_DIM].reshape(depth, r, N_PAIRS, LANES)
    rope = w[..., HEAD_DIM:].reshape(depth, r, N_PAIRS, 2 * MLA_ROPE_DIM)
    pad = jnp.zeros((depth, r, N_PAIRS, LANES - 2 * MLA_ROPE_DIM), w.dtype)
    return jnp.concatenate([nope, rope, pad], axis=-1).reshape(depth, r, MLA_QK_W).astype(BF16)


def _layout_w_kv_up(w_kv_up):
    depth, r, _ = w_kv_up.shape
    w = w_kv_up.reshape(depth, r, MLA_HEADS, 2 * HEAD_DIM)
    k = w[..., :HEAD_DIM].reshape(depth, r, MLA_WIDTH)
    v = w[..., HEAD_DIM:].reshape(depth, r, MLA_WIDTH)
    return jnp.concatenate([k, v], axis=-1).astype(BF16)


def kernel(x, c, positions, w_ada, b_ada, g_mix_norm, w_in, g_q_lat, w_q_up, g_kv_lat, w_kv_up,
           w_sb_out, w_mla_out, w_mix_out, g_mlp_norm, w_up, w_down, g_final):
    batch, seq, d = x.shape
    depth = w_in.shape[0]
    t = batch * seq

    w_in_l = _layout_w_in(w_in)
    w_q_l = _layout_w_q_up(w_q_up)
    w_kv_l = _layout_w_kv_up(w_kv_up)
    w_sb_b, w_mla_b, w_mix_b = (w.astype(BF16) for w in (w_sb_out, w_mla_out, w_mix_out))
    w_up_b, w_dn_b = w_up.astype(BF16), w_down.astype(BF16)

    mod_all = _mod_call(c, w_ada, b_ada).reshape(depth, batch, 1, N_MOD * d)
    cos_t, sin_t = _rope_tables(positions)

    def per_batch(a):
        return a.reshape(N_PAIRS, batch, seq, a.shape[-1])

    xt = x.reshape(t, d)
    for l in range(depth):
        mod = mod_all[l]
        q_sb, k_sb, v_sb, q_m, k_m, v_m, sg_sb, sg_mla = _in_proj_call(
            xt, mod, g_mix_norm[l].reshape(1, d), w_in_l,
            g_q_lat[l].reshape(1, -1), w_q_l, g_kv_lat[l].reshape(1, -1), w_kv_l,
            cos_t, sin_t, seq, l)
        o_sb = _sb_attn_call(per_batch(q_sb), per_batch(k_sb), per_batch(v_sb))
        o_mla = _mla_attn_call(per_batch(q_m), per_batch(k_m), per_batch(v_m))
        xt = _post_call(xt, o_sb.reshape(N_PAIRS, t, LANES), o_mla.reshape(N_PAIRS, t, LANES),
                        sg_sb, sg_mla, mod,
                        w_sb_b, w_mla_b, w_mix_b, g_mlp_norm[l].reshape(1, d),
                        w_up_b, w_dn_b, g_final.reshape(1, d), seq, l, l == depth - 1)
    return xt.reshape(batch, seq, d)
```

```python
import functools
import math

import jax
import jax.numpy as jnp
from jax import lax
from jax.experimental import pallas as pl
from jax.experimental.pallas import tpu as pltpu

D_MODEL = 1024
SB_HEADS = 8
HEAD_DIM = 64
SB_WIDTH = SB_HEADS * HEAD_DIM
MLA_HEADS = 8
MLA_ROPE_DIM = 32
MLA_QK_DIM = HEAD_DIM + MLA_ROPE_DIM
MLA_Q_RANK = 384
MLA_KV_RANK = 256
MLA_WIDTH = MLA_HEADS * HEAD_DIM
D_FF = 4 * D_MODEL
ROPE_THETA = 10000.0
NORM_EPS = 1e-6
N_MOD = 6
LOG2E = math.log2(math.e)

LANES = 128
PAIR_W = 2 * LANES
N_PAIRS = SB_HEADS // 2
MLA_QK_W = N_PAIRS * PAIR_W

OFF_QKV = 0
OFF_QLAT = 3 * SB_WIDTH
OFF_KVLAT = OFF_QLAT + MLA_Q_RANK
OFF_KROPE = OFF_KVLAT + MLA_KV_RANK
OFF_GATES = OFF_KROPE + LANES
IN_W = OFF_GATES + 2 * D_MODEL

ROW_TILE = 512
ATTN_TILE = 256
LOOP_UNROLL = 2
VMEM_LIMIT = 56 * 1024 * 1024
MASKED = -1e30

F32 = jnp.float32
BF16 = jnp.bfloat16


def _resident(shape, layer=None):
    nd = len(shape)
    if layer is None:
        return pl.BlockSpec(shape, lambda *_: (0,) * nd, pipeline_mode=pl.Buffered(1))
    return pl.BlockSpec((None,) + tuple(shape), lambda *_: (layer,) + (0,) * nd,
                        pipeline_mode=pl.Buffered(1))


def _rms(x, g):
    return x * lax.rsqrt(jnp.mean(x * x, axis=-1, keepdims=True) + NORM_EPS) * g


def _dot(a, b):
    return jnp.dot(a, b, preferred_element_type=F32)


def _dot_nt(a, b):
    return lax.dot_general(a, b, (((1,), (1,)), ((), ())), preferred_element_type=F32)


def _mod_kernel(c_ref, w_ref, b_ref, o_ref):
    c = c_ref[...]
    c_act = c * (1.0 / (1.0 + jnp.exp(-c)))
    o_ref[0] = _dot(c_act.astype(BF16), w_ref[0].astype(BF16)) + b_ref[0]


def _mod_call(c, w_ada, b_ada):
    depth, d, n = w_ada.shape
    batch = c.shape[0]
    tn = 1536
    return pl.pallas_call(
        _mod_kernel,
        out_shape=jax.ShapeDtypeStruct((depth, batch, n), F32),
        grid=(depth, n // tn),
        in_specs=[
            pl.BlockSpec((batch, d), lambda l, j: (0, 0)),
            pl.BlockSpec((1, d, tn), lambda l, j: (l, 0, j)),
            pl.BlockSpec((1, 1, tn), lambda l, j: (l, 0, j)),
        ],
        out_specs=pl.BlockSpec((1, batch, tn), lambda l, j: (l, 0, j)),
        compiler_params=pltpu.CompilerParams(
            dimension_semantics=("parallel", "parallel"), vmem_limit_bytes=VMEM_LIMIT),
        name="adaln_mod",
    )(c, w_ada, b_ada.reshape(depth, 1, n))


def _rope_table_kernel(pos_ref, inv_ref, sign_ref, cos_ref, sin_ref):
    ang = pos_ref[...].astype(F32) * inv_ref[...]
    cos_ref[...] = jnp.cos(ang)
    sin_ref[...] = jnp.sin(ang) * sign_ref[...]


def _rope_tables(positions):
    t = positions.size
    half = MLA_ROPE_DIM // 2
    inv_freq = 1.0 / (ROPE_THETA ** (jnp.arange(0, MLA_ROPE_DIM, 2, dtype=F32) / MLA_ROPE_DIM))
    lane = jnp.arange(LANES)
    inv_lane = inv_freq[lane % half].reshape(1, LANES)
    sign_lane = jnp.where((lane % MLA_ROPE_DIM) < half, -1.0, 1.0).astype(F32).reshape(1, LANES)
    ts = 2048
    return pl.pallas_call(
        _rope_table_kernel,
        out_shape=(jax.ShapeDtypeStruct((t, LANES), F32),) * 2,
        grid=(t // ts,),
        in_specs=[
            pl.BlockSpec((ts, 1), lambda i: (i, 0)),
            pl.BlockSpec((1, LANES), lambda i: (0, 0)),
            pl.BlockSpec((1, LANES), lambda i: (0, 0)),
        ],
        out_specs=(pl.BlockSpec((ts, LANES), lambda i: (i, 0)),) * 2,
        compiler_params=pltpu.CompilerParams(dimension_semantics=("parallel",)),
        name="rope_tables",
    )(positions.reshape(t, 1), inv_lane, sign_lane)


def _rope(t, cos, sin_signed):
    lane = lax.broadcasted_iota(jnp.int32, (1, LANES), 1)
    low = (lane % MLA_ROPE_DIM) < (MLA_ROPE_DIM // 2)
    half = MLA_ROPE_DIM // 2
    swapped = jnp.where(low, pltpu.roll(t, LANES - half, 1), pltpu.roll(t, half, 1))
    return t * cos + swapped * sin_signed


def _in_proj_kernel(x_ref, mod_ref, g_ref, w_ref, gq_ref, wq_ref, gkv_ref, wkv_ref,
                    cos_ref, sin_ref,
                    qsb_ref, ksb_ref, vsb_ref, qm_ref, km_ref, vm_ref, sgs_ref, sgm_ref):
    mod = mod_ref[0]
    shift1 = mod[:, 0:D_MODEL]
    scale1 = mod[:, D_MODEL:2 * D_MODEL]
    h = _rms(x_ref[...], g_ref[...]) * (1.0 + scale1) + shift1
    hb = h.astype(BF16)

    lat = _dot(hb, w_ref[:, OFF_QLAT:OFF_GATES])
    q_lat = lat[:, 0:MLA_Q_RANK]
    kv_lat = lat[:, MLA_Q_RANK:MLA_Q_RANK + MLA_KV_RANK]
    k_rope = lat[:, MLA_Q_RANK + MLA_KV_RANK:]

    qkv = _dot(hb, w_ref[:, OFF_QKV:OFF_QLAT])
    for p in range(N_PAIRS):
        lo = p * LANES
        qsb_ref[p] = (qkv[:, lo:lo + LANES] * (HEAD_DIM ** -0.5 * LOG2E)).astype(BF16)
        ksb_ref[p] = qkv[:, SB_WIDTH + lo:SB_WIDTH + lo + LANES].astype(BF16)
        vsb_ref[p] = qkv[:, 2 * SB_WIDTH + lo:2 * SB_WIDTH + lo + LANES].astype(BF16)

    gates = _dot(hb, w_ref[:, OFF_GATES:IN_W])
    sig = 1.0 / (1.0 + jnp.exp(-gates))
    sgs_ref[...] = sig[:, 0:D_MODEL].astype(BF16)
    sgm_ref[...] = sig[:, D_MODEL:2 * D_MODEL].astype(BF16)

    cos = cos_ref[...]
    sin = sin_ref[...]

    q = _dot(_rms(q_lat, gq_ref[...]).astype(BF16), wq_ref[...]) * (MLA_QK_DIM ** -0.5 * LOG2E)
    kv = _dot(_rms(kv_lat, gkv_ref[...]).astype(BF16), wkv_ref[...])
    k_pe = _rope(k_rope, cos, sin).astype(BF16)
    for p in range(N_PAIRS):
        lo = p * PAIR_W
        qm_ref[p, :, 0:LANES] = q[:, lo:lo + LANES].astype(BF16)
        qm_ref[p, :, LANES:PAIR_W] = _rope(q[:, lo + LANES:lo + PAIR_W], cos, sin).astype(BF16)
        km_ref[p, :, 0:LANES] = kv[:, p * LANES:(p + 1) * LANES].astype(BF16)
        km_ref[p, :, LANES:PAIR_W] = k_pe
        vm_ref[p] = kv[:, MLA_WIDTH + p * LANES:MLA_WIDTH + (p + 1) * LANES].astype(BF16)


def _in_proj_call(x, mod, g, w_in, gq, wq, gkv, wkv, cos_t, sin_t, seq, layer):
    t = x.shape[0]
    tm = ROW_TILE
    tiles_per_seq = seq // tm
    row = lambda i: (i, 0)
    pair_widths = (LANES, LANES, LANES, PAIR_W, PAIR_W, LANES)
    pair_shapes = tuple(jax.ShapeDtypeStruct((N_PAIRS, t, w), BF16) for w in pair_widths)
    pair_specs = tuple(pl.BlockSpec((N_PAIRS, tm, w), lambda i: (0, i, 0)) for w in pair_widths)
    gate_shape = jax.ShapeDtypeStruct((t, D_MODEL), BF16)
    gate_spec = pl.BlockSpec((tm, D_MODEL), row)
    return pl.pallas_call(
        _in_proj_kernel,
        out_shape=pair_shapes + (gate_shape, gate_shape),
        grid=(t // tm,),
        in_specs=[
            pl.BlockSpec((tm, D_MODEL), row),
            pl.BlockSpec((1, 1, N_MOD * D_MODEL), lambda i: (i // tiles_per_seq, 0, 0)),
            _resident((1, D_MODEL)),
            _resident((D_MODEL, IN_W), layer),
            _resident((1, MLA_Q_RANK)),
            _resident((MLA_Q_RANK, MLA_QK_W), layer),
            _resident((1, MLA_KV_RANK)),
            _resident((MLA_KV_RANK, 2 * MLA_WIDTH), layer),
            pl.BlockSpec((tm, LANES), row),
            pl.BlockSpec((tm, LANES), row),
        ],
        out_specs=pair_specs + (gate_spec, gate_spec),
        compiler_params=pltpu.CompilerParams(
            dimension_semantics=("parallel",), vmem_limit_bytes=VMEM_LIMIT),
        name="in_proj",
    )(x, mod, g, w_in, gq, wq, gkv, wkv, cos_t, sin_t)


def _causal_blocks(nq, depth):
    blocks = [(i, i) for i in range(nq)]
    blocks += [(i, j) for i in range(1, nq) for j in range(i - 1, -1, -1)]
    pad = depth - 1
    table = [blocks[0]] * pad + blocks + [blocks[-1]] * pad
    return (jnp.asarray([e[0] for e in table], jnp.int32),
            jnp.asarray([e[1] for e in table], jnp.int32), len(blocks))


def _block_offset(i):
    return pl.multiple_of(i * ATTN_TILE, ATTN_TILE)


def _run_pipeline(stages, it_ref, jt_ref, nq, n_blocks):
    depth = len(stages)

    def step(diag):
        def body(t, carry):
            for p in range(N_PAIRS):
                for k in range(depth - 1, -1, -1):
                    e = t - k + depth - 1
                    stages[k](p, it_ref[e], jt_ref[e], diag)
            return carry
        return body

    n_steps = n_blocks + depth - 1
    assert nq % LOOP_UNROLL == 0
    looped = nq + (n_steps - nq) // LOOP_UNROLL * LOOP_UNROLL
    lax.fori_loop(0, nq, step(True), 0, unroll=LOOP_UNROLL)
    lax.fori_loop(nq, looped, step(False), 0, unroll=LOOP_UNROLL)
    for t in range(looped, n_steps):
        step(False)(t, 0)


def _sb_attn_kernel(it_ref, jt_ref, q_ref, k_ref, v_ref, o_ref, later_ref, *scratch,
                    nq, n_blocks):
    blk = ATTN_TILE
    per_pair = [scratch[7 * p:7 * p + 7] for p in range(N_PAIRS)]
    acc_refs, r_refs, ls_refs, lf_refs, rs0_refs, rs1_refs, w_refs = zip(*per_pair)
    lane = lax.broadcasted_iota(jnp.int32, (1, LANES), 1)
    first = lane < HEAD_DIM
    row = lax.broadcasted_iota(jnp.int32, (blk, blk), 0)
    col = lax.broadcasted_iota(jnp.int32, (blk, blk), 1)
    row2 = lax.broadcasted_iota(jnp.int32, (2 * blk, blk), 0) & (blk - 1)
    col2 = lax.broadcasted_iota(jnp.int32, (2 * blk, blk), 1)
    strict = col2 < row2
    later_ref[...] = jnp.where(row > col, 1.0, 0.0).astype(BF16)
    for p in range(N_PAIRS):
        acc_refs[p][...] = jnp.zeros_like(acc_refs[p])
        r_refs[p][...] = jnp.zeros_like(r_refs[p])
        ls_refs[p][...] = jnp.full_like(ls_refs[p], MASKED)
        lf_refs[p][...] = jnp.zeros_like(lf_refs[p])
        rs0_refs[p][...] = jnp.zeros_like(rs0_refs[p])
        rs1_refs[p][...] = jnp.zeros_like(rs1_refs[p])
        w_refs[p][...] = jnp.zeros_like(w_refs[p])

    def scores(p, i, j, diag):
        q = q_ref[p, pl.ds(_block_offset(i), blk), :]
        zero = jnp.zeros_like(q)
        q2 = jnp.concatenate([jnp.where(first, q, zero), jnp.where(first, zero, q)], axis=0)
        z = _dot_nt(q2, k_ref[p, pl.ds(_block_offset(j), blk), :])
        neg_part = jnp.minimum(z, 0.0)
        neg_relu = neg_part - z
        softplus = jnp.log2(1.0 + jnp.exp2(neg_part + neg_relu))
        ls = neg_part - softplus
        lf = neg_relu - softplus
        if diag:
            ls = jnp.where(strict, ls, MASKED)
            lf = jnp.where(strict, lf, 0.0)
        rs = jnp.sum(lf, axis=1, keepdims=True)
        ls_refs[p][...] = ls
        lf_refs[p][...] = lf.astype(BF16)
        rs0_refs[p][...] = jnp.where(first, rs[0:blk], rs[blk:2 * blk])

    def weights(p, i, j, diag):
        cum = _dot(lf_refs[p][...], later_ref[...])
        w_refs[p][...] = jnp.exp2(ls_refs[p][...] + cum).astype(BF16)
        rs1_refs[p][...] = rs0_refs[p][...]

    def accumulate(p, i, j, diag):
        pv = _dot(w_refs[p][...], v_ref[p, pl.ds(_block_offset(j), blk), :])
        pv = jnp.where(first, pv[0:blk], pv[blk:2 * blk])
        r = r_refs[p][i]
        acc_refs[p][i] = acc_refs[p][i] + jnp.exp2(r) * pv
        r_refs[p][i] = r + rs1_refs[p][...]

    _run_pipeline((scores, weights, accumulate), it_ref, jt_ref, nq, n_blocks)
    for p in range(N_PAIRS):
        for i in range(nq):
            o_ref[p, i * blk:(i + 1) * blk, :] = acc_refs[p][i].astype(o_ref.dtype)


def _sb_attn_call(q, k, v):
    n_pairs, b, s, _ = q.shape
    blk = ATTN_TILE
    nq = s // blk
    it, jt, n_blocks = _causal_blocks(nq, 3)
    spec = pl.BlockSpec((n_pairs, None, s, LANES), lambda bi, *_: (0, bi, 0, 0))
    grid_spec = pltpu.PrefetchScalarGridSpec(
        num_scalar_prefetch=2,
        grid=(b,),
        in_specs=[spec, spec, spec],
        out_specs=spec,
        scratch_shapes=[pltpu.VMEM((blk, blk), BF16)] + n_pairs * [
            pltpu.VMEM((nq, blk, LANES), F32),
            pltpu.VMEM((nq, blk, LANES), F32),
            pltpu.VMEM((2 * blk, blk), F32),
            pltpu.VMEM((2 * blk, blk), BF16),
            pltpu.VMEM((blk, LANES), F32),
            pltpu.VMEM((blk, LANES), F32),
            pltpu.VMEM((2 * blk, blk), BF16),
        ],
    )
    return pl.pallas_call(
        functools.partial(_sb_attn_kernel, nq=nq, n_blocks=n_blocks),
        out_shape=jax.ShapeDtypeStruct(q.shape, BF16),
        grid_spec=grid_spec,
        compiler_params=pltpu.CompilerParams(
            dimension_semantics=("parallel",), vmem_limit_bytes=VMEM_LIMIT),
        name="sb_attn",
    )(it, jt, q, k, v)


def _mla_attn_kernel(it_ref, jt_ref, q_ref, k_ref, v_ref, o_ref, vblk_ref, *scratch,
                     nq, n_blocks):
    blk = ATTN_TILE
    per_pair = [scratch[4 * p:4 * p + 4] for p in range(N_PAIRS)]
    acc_refs, m_refs, alpha_refs, p_refs = zip(*per_pair)
    lane = lax.broadcasted_iota(jnp.int32, (1, LANES), 1)
    first = lane < HEAD_DIM
    lane2 = lax.broadcasted_iota(jnp.int32, (1, PAIR_W), 1)
    own_a = (lane2 < HEAD_DIM) | ((lane2 >= LANES) & (lane2 < LANES + MLA_ROPE_DIM))
    own_b = ((lane2 >= HEAD_DIM) & (lane2 < LANES)) | (
        (lane2 >= LANES + MLA_ROPE_DIM) & (lane2 < LANES + 2 * MLA_ROPE_DIM))
    row2 = lax.broadcasted_iota(jnp.int32, (2 * blk, blk), 0) & (blk - 1)
    col2 = lax.broadcasted_iota(jnp.int32, (2 * blk, blk), 1)
    causal = col2 <= row2
    sum_a = jnp.broadcast_to(jnp.where(first, 1.0, 0.0).astype(BF16), (blk, LANES))
    sum_b = jnp.broadcast_to(jnp.where(first, 0.0, 1.0).astype(BF16), (blk, LANES))
    for p in range(N_PAIRS):
        for j in range(nq):
            v = v_ref[p, j * blk:(j + 1) * blk, :]
            zero = jnp.zeros_like(v)
            lo = 2 * j * blk
            vblk_ref[p, lo:lo + blk, 0:LANES] = jnp.where(first, v, zero)
            vblk_ref[p, lo:lo + blk, LANES:PAIR_W] = sum_a
            vblk_ref[p, lo + blk:lo + 2 * blk, 0:LANES] = jnp.where(first, zero, v)
            vblk_ref[p, lo + blk:lo + 2 * blk, LANES:PAIR_W] = sum_b
        acc_refs[p][...] = jnp.zeros_like(acc_refs[p])
        m_refs[p][...] = jnp.full_like(m_refs[p], MASKED)
        alpha_refs[p][...] = jnp.zeros_like(alpha_refs[p])
        p_refs[p][...] = jnp.zeros_like(p_refs[p])

    def probabilities(p, i, j, diag):
        q = q_ref[p, pl.ds(_block_offset(i), blk), :]
        zero = jnp.zeros_like(q)
        q2 = jnp.concatenate([jnp.where(own_a, q, zero), jnp.where(own_b, q, zero)], axis=0)
        sc = _dot_nt(q2, k_ref[p, pl.ds(_block_offset(j), blk), :])
        if diag:
            sc = jnp.where(causal, sc, MASKED)
        m_old = m_refs[p][i]
        m_new = jnp.maximum(m_old, jnp.max(sc, axis=1, keepdims=True))
        alpha_refs[p][...] = jnp.exp2(m_old - m_new)
        m_refs[p][i] = m_new
        prob = jnp.exp2(sc - jnp.concatenate([m_new, m_new], axis=1)).astype(BF16)
        p_refs[p][:, 0:blk] = prob[0:blk]
        p_refs[p][:, blk:2 * blk] = prob[blk:2 * blk]

    def accumulate(p, i, j, diag):
        pv = _dot(p_refs[p][...], vblk_ref[p, pl.ds(pl.multiple_of(2 * j * blk, 2 * blk), 2 * blk), :])
        alpha = alpha_refs[p][...]
        alpha = jnp.where(first, alpha[0:blk], alpha[blk:2 * blk])
        acc_refs[p][i] = acc_refs[p][i] * jnp.concatenate([alpha, alpha], axis=1) + pv

    _run_pipeline((probabilities, accumulate), it_ref, jt_ref, nq, n_blocks)
    for p in range(N_PAIRS):
        for i in range(nq):
            a = acc_refs[p][i]
            o_ref[p, i * blk:(i + 1) * blk, :] = (a[:, 0:LANES] / a[:, LANES:PAIR_W]).astype(o_ref.dtype)


def _mla_attn_call(q, k, v):
    n_pairs, b, s, _ = v.shape
    blk = ATTN_TILE
    nq = s // blk
    it, jt, n_blocks = _causal_blocks(nq, 2)
    qk_spec = pl.BlockSpec((n_pairs, None, s, PAIR_W), lambda bi, *_: (0, bi, 0, 0))
    v_spec = pl.BlockSpec((n_pairs, None, s, LANES), lambda bi, *_: (0, bi, 0, 0))
    grid_spec = pltpu.PrefetchScalarGridSpec(
        num_scalar_prefetch=2,
        grid=(b,),
        in_specs=[qk_spec, qk_spec, v_spec],
        out_specs=v_spec,
        scratch_shapes=[pltpu.VMEM((n_pairs, 2 * s, PAIR_W), BF16)] + n_pairs * [
            pltpu.VMEM((nq, blk, PAIR_W), F32),
            pltpu.VMEM((nq, 2 * blk, LANES), F32),
            pltpu.VMEM((2 * blk, LANES), F32),
            pltpu.VMEM((blk, 2 * blk), BF16),
        ],
    )
    return pl.pallas_call(
        functools.partial(_mla_attn_kernel, nq=nq, n_blocks=n_blocks),
        out_shape=jax.ShapeDtypeStruct(v.shape, BF16),
        grid_spec=grid_spec,
        compiler_params=pltpu.CompilerParams(
            dimension_semantics=("parallel",), vmem_limit_bytes=VMEM_LIMIT),
        name="mla_attn",
    )(it, jt, q, k, v)


def _post_kernel(x_ref, osb_ref, omla_ref, sgs_ref, sgm_ref, mod_ref,
                 wsb_ref, wmla_ref, wmix_ref, g2_ref, wup_ref, wdn_ref, gf_ref,
                 o_ref, *, final):
    mod = mod_ref[0]
    gate1 = mod[:, 2 * D_MODEL:3 * D_MODEL]
    shift2 = mod[:, 3 * D_MODEL:4 * D_MODEL]
    scale2 = mod[:, 4 * D_MODEL:5 * D_MODEL]
    gate2 = mod[:, 5 * D_MODEL:6 * D_MODEL]

    o_sb = _dot(jnp.concatenate([osb_ref[p] for p in range(N_PAIRS)], axis=1), wsb_ref[...])
    o_mla = _dot(jnp.concatenate([omla_ref[p] for p in range(N_PAIRS)], axis=1), wmla_ref[...])
    merged = sgs_ref[...].astype(F32) * o_sb + sgm_ref[...].astype(F32) * o_mla
    x = x_ref[...] + gate1 * _dot(merged.astype(BF16), wmix_ref[...])

    h = _rms(x, g2_ref[...]) * (1.0 + scale2) + shift2
    u = jnp.maximum(_dot(h.astype(BF16), wup_ref[...]), 0.0)
    x = x + gate2 * _dot((u * u).astype(BF16), wdn_ref[...])
    if final:
        x = _rms(x, gf_ref[...])
    o_ref[...] = x


def _post_call(x, o_sb, o_mla, sg_sb, sg_mla, mod, w_sb, w_mla, w_mix, g2, w_up, w_dn, g_final,
               seq, layer, final):
    t = x.shape[0]
    tm = ROW_TILE
    tiles_per_seq = seq // tm
    row = lambda i: (i, 0)
    pair_spec = pl.BlockSpec((N_PAIRS, tm, LANES), lambda i: (0, i, 0))
    return pl.pallas_call(
        functools.partial(_post_kernel, final=final),
        out_shape=jax.ShapeDtypeStruct((t, D_MODEL), F32),
        grid=(t // tm,),
        in_specs=[
            pl.BlockSpec((tm, D_MODEL), row),
            pair_spec,
            pair_spec,
            pl.BlockSpec((tm, D_MODEL), row),
            pl.BlockSpec((tm, D_MODEL), row),
            pl.BlockSpec((1, 1, N_MOD * D_MODEL), lambda i: (i // tiles_per_seq, 0, 0)),
            _resident((SB_WIDTH, D_MODEL), layer),
            _resident((MLA_WIDTH, D_MODEL), layer),
            _resident((D_MODEL, D_MODEL), layer),
            _resident((1, D_MODEL)),
            _resident((D_MODEL, D_FF), layer),
            _resident((D_FF, D_MODEL), layer),
            _resident((1, D_MODEL)),
        ],
        out_specs=pl.BlockSpec((tm, D_MODEL), row),
        compiler_params=pltpu.CompilerParams(
            dimension_semantics=("parallel",), vmem_limit_bytes=VMEM_LIMIT),
        name="post_final" if final else "post",
    )(x, o_sb, o_mla, sg_sb, sg_mla, mod, w_sb, w_mla, w_mix, g2, w_up, w_dn, g_final)


def _layout_w_in(w_in):
    depth, d, _ = w_in.shape
    head = w_in[..., 0:OFF_KROPE]
    k_rope = w_in[..., OFF_KROPE:OFF_KROPE + MLA_ROPE_DIM]
    pad = jnp.zeros((depth, d, LANES - 2 * MLA_ROPE_DIM), w_in.dtype)
    gates = w_in[..., OFF_KROPE + MLA_ROPE_DIM:]
    return jnp.concatenate([head, k_rope, k_rope, pad, gates], axis=-1).astype(BF16)


def _layout_w_q_up(w_q_up):
    depth, r, _ = w_q_up.shape
    w = w_q_up.reshape(depth, r, N_PAIRS, 2, MLA_QK_DIM)
    nope = w[..., :HEAD_DIM].reshape(depth, r, N_PAIRS, LANES)
    rope = w[..., HEAD_DIM:].reshape(depth, r, N_PAIRS, 2 * MLA_ROPE_DIM)
    pad = jnp.zeros((depth, r, N_PAIRS, LANES - 2 * MLA_ROPE_DIM), w.dtype)
    return jnp.concatenate([nope, rope, pad], axis=-1).reshape(depth, r, MLA_QK_W).astype(BF16)


def _layout_w_kv_up(w_kv_up):
    depth, r, _ = w_kv_up.shape
    w = w_kv_up.reshape(depth, r, MLA_HEADS, 2 * HEAD_DIM)
    k = w[..., :HEAD_DIM].reshape(depth, r, MLA_WIDTH)
    v = w[..., HEAD_DIM:].reshape(depth, r, MLA_WIDTH)
    return jnp.concatenate([k, v], axis=-1).astype(BF16)


def kernel(x, c, positions, w_ada, b_ada, g_mix_norm, w_in, g_q_lat, w_q_up, g_kv_lat, w_kv_up,
           w_sb_out, w_mla_out, w_mix_out, g_mlp_norm, w_up, w_down, g_final):
    batch, seq, d = x.shape
    depth = w_in.shape[0]
    t = batch * seq

    w_in_l = _layout_w_in(w_in)
    w_q_l = _layout_w_q_up(w_q_up)
    w_kv_l = _layout_w_kv_up(w_kv_up)
    w_sb_b, w_mla_b, w_mix_b = (w.astype(BF16) for w in (w_sb_out, w_mla_out, w_mix_out))
    w_up_b, w_dn_b = w_up.astype(BF16), w_down.astype(BF16)

    mod_all = _mod_call(c, w_ada, b_ada).reshape(depth, batch, 1, N_MOD * d)
    cos_t, sin_t = _rope_tables(positions)

    def per_batch(a):
        return a.reshape(N_PAIRS, batch, seq, a.shape[-1])

    xt = x.reshape(t, d)
    for l in range(depth):
        mod = mod_all[l]
        q_sb, k_sb, v_sb, q_m, k_m, v_m, sg_sb, sg_mla = _in_proj_call(
            xt, mod, g_mix_norm[l].reshape(1, d), w_in_l,
            g_q_lat[l].reshape(1, -1), w_q_l, g_kv_lat[l].reshape(1, -1), w_kv_l,
            cos_t, sin_t, seq, l)
        o_sb = _sb_attn_call(per_batch(q_sb), per_batch(k_sb), per_batch(v_sb))
        o_mla = _mla_attn_call(per_batch(q_m), per_batch(k_m), per_batch(v_m))
        xt = _post_call(xt, o_sb.reshape(N_PAIRS, t, LANES), o_mla.reshape(N_PAIRS, t, LANES),
                        sg_sb, sg_mla, mod,
                        w_sb_b, w_mla_b, w_mix_b, g_mlp_norm[l].reshape(1, d),
                        w_up_b, w_dn_b, g_final.reshape(1, d), seq, l, l == depth - 1)
    return xt.reshape(batch, seq, d)
```

```python
import functools
import math

import jax
import jax.numpy as jnp
from jax import lax
from jax.experimental import pallas as pl
from jax.experimental.pallas import tpu as pltpu

D_MODEL = 1024
SB_HEADS = 8
HEAD_DIM = 64
SB_WIDTH = SB_HEADS * HEAD_DIM
MLA_HEADS = 8
MLA_ROPE_DIM = 32
MLA_QK_DIM = HEAD_DIM + MLA_ROPE_DIM
MLA_Q_RANK = 384
MLA_KV_RANK = 256
MLA_WIDTH = MLA_HEADS * HEAD_DIM
D_FF = 4 * D_MODEL
ROPE_THETA = 10000.0
NORM_EPS = 1e-6
N_MOD = 6
LOG2E = math.log2(math.e)

LANES = 128
PAIR_W = 2 * LANES
N_PAIRS = SB_HEADS // 2
MLA_QK_W = N_PAIRS * PAIR_W

OFF_QKV = 0
OFF_QLAT = 3 * SB_WIDTH
OFF_KVLAT = OFF_QLAT + MLA_Q_RANK
OFF_KROPE = OFF_KVLAT + MLA_KV_RANK
OFF_GATES = OFF_KROPE + LANES
IN_W = OFF_GATES + 2 * D_MODEL

ROW_TILE = 512
ATTN_TILE = 256
LOOP_UNROLL = 4
VMEM_LIMIT = 56 * 1024 * 1024
MASKED = -1e30

F32 = jnp.float32
BF16 = jnp.bfloat16


def _resident(shape, layer=None):
    nd = len(shape)
    if layer is None:
        return pl.BlockSpec(shape, lambda *_: (0,) * nd, pipeline_mode=pl.Buffered(1))
    return pl.BlockSpec((None,) + tuple(shape), lambda *_: (layer,) + (0,) * nd,
                        pipeline_mode=pl.Buffered(1))


def _rms(x, g):
    return x * lax.rsqrt(jnp.mean(x * x, axis=-1, keepdims=True) + NORM_EPS) * g


def _dot(a, b):
    return jnp.dot(a, b, preferred_element_type=F32)


def _dot_nt(a, b):
    return lax.dot_general(a, b, (((1,), (1,)), ((), ())), preferred_element_type=F32)


def _mod_kernel(c_ref, w_ref, b_ref, o_ref):
    c = c_ref[...]
    c_act = c * (1.0 / (1.0 + jnp.exp(-c)))
    o_ref[0] = _dot(c_act.astype(BF16), w_ref[0].astype(BF16)) + b_ref[0]


def _mod_call(c, w_ada, b_ada):
    depth, d, n = w_ada.shape
    batch = c.shape[0]
    tn = 1536
    return pl.pallas_call(
        _mod_kernel,
        out_shape=jax.ShapeDtypeStruct((depth, batch, n), F32),
        grid=(depth, n // tn),
        in_specs=[
            pl.BlockSpec((batch, d), lambda l, j: (0, 0)),
            pl.BlockSpec((1, d, tn), lambda l, j: (l, 0, j)),
            pl.BlockSpec((1, 1, tn), lambda l, j: (l, 0, j)),
        ],
        out_specs=pl.BlockSpec((1, batch, tn), lambda l, j: (l, 0, j)),
        compiler_params=pltpu.CompilerParams(
            dimension_semantics=("parallel", "parallel"), vmem_limit_bytes=VMEM_LIMIT),
        name="adaln_mod",
    )(c, w_ada, b_ada.reshape(depth, 1, n))


def _rope_table_kernel(pos_ref, inv_ref, sign_ref, cos_ref, sin_ref):
    ang = pos_ref[...].astype(F32) * inv_ref[...]
    cos_ref[...] = jnp.cos(ang)
    sin_ref[...] = jnp.sin(ang) * sign_ref[...]


def _rope_tables(positions):
    t = positions.size
    half = MLA_ROPE_DIM // 2
    inv_freq = 1.0 / (ROPE_THETA ** (jnp.arange(0, MLA_ROPE_DIM, 2, dtype=F32) / MLA_ROPE_DIM))
    lane = jnp.arange(LANES)
    inv_lane = inv_freq[lane % half].reshape(1, LANES)
    sign_lane = jnp.where((lane % MLA_ROPE_DIM) < half, -1.0, 1.0).astype(F32).reshape(1, LANES)
    ts = 2048
    return pl.pallas_call(
        _rope_table_kernel,
        out_shape=(jax.ShapeDtypeStruct((t, LANES), F32),) * 2,
        grid=(t // ts,),
        in_specs=[
            pl.BlockSpec((ts, 1), lambda i: (i, 0)),
            pl.BlockSpec((1, LANES), lambda i: (0, 0)),
            pl.BlockSpec((1, LANES), lambda i: (0, 0)),
        ],
        out_specs=(pl.BlockSpec((ts, LANES), lambda i: (i, 0)),) * 2,
        compiler_params=pltpu.CompilerParams(dimension_semantics=("parallel",)),
        name="rope_tables",
    )(positions.reshape(t, 1), inv_lane, sign_lane)


def _rope(t, cos, sin_signed):
    lane = lax.broadcasted_iota(jnp.int32, (1, LANES), 1)
    low = (lane % MLA_ROPE_DIM) < (MLA_ROPE_DIM // 2)
    half = MLA_ROPE_DIM // 2
    swapped = jnp.where(low, pltpu.roll(t, LANES - half, 1), pltpu.roll(t, half, 1))
    return t * cos + swapped * sin_signed


def _in_proj_kernel(x_ref, mod_ref, g_ref, w_ref, gq_ref, wq_ref, gkv_ref, wkv_ref,
                    cos_ref, sin_ref,
                    qsb_ref, ksb_ref, vsb_ref, qm_ref, km_ref, vm_ref, sgs_ref, sgm_ref):
    mod = mod_ref[0]
    shift1 = mod[:, 0:D_MODEL]
    scale1 = mod[:, D_MODEL:2 * D_MODEL]
    h = _rms(x_ref[...], g_ref[...]) * (1.0 + scale1) + shift1
    hb = h.astype(BF16)

    lat = _dot(hb, w_ref[:, OFF_QLAT:OFF_GATES])
    q_lat = lat[:, 0:MLA_Q_RANK]
    kv_lat = lat[:, MLA_Q_RANK:MLA_Q_RANK + MLA_KV_RANK]
    k_rope = lat[:, MLA_Q_RANK + MLA_KV_RANK:]

    qkv = _dot(hb, w_ref[:, OFF_QKV:OFF_QLAT])
    for p in range(N_PAIRS):
        lo = p * LANES
        qsb_ref[p] = (qkv[:, lo:lo + LANES] * (HEAD_DIM ** -0.5 * LOG2E)).astype(BF16)
        ksb_ref[p] = qkv[:, SB_WIDTH + lo:SB_WIDTH + lo + LANES].astype(BF16)
        vsb_ref[p] = qkv[:, 2 * SB_WIDTH + lo:2 * SB_WIDTH + lo + LANES].astype(BF16)

    gates = _dot(hb, w_ref[:, OFF_GATES:IN_W])
    sig = 1.0 / (1.0 + jnp.exp(-gates))
    sgs_ref[...] = sig[:, 0:D_MODEL].astype(BF16)
    sgm_ref[...] = sig[:, D_MODEL:2 * D_MODEL].astype(BF16)

    cos = cos_ref[...]
    sin = sin_ref[...]

    q = _dot(_rms(q_lat, gq_ref[...]).astype(BF16), wq_ref[...]) * (MLA_QK_DIM ** -0.5 * LOG2E)
    kv = _dot(_rms(kv_lat, gkv_ref[...]).astype(BF16), wkv_ref[...])
    k_pe = _rope(k_rope, cos, sin).astype(BF16)
    for p in range(N_PAIRS):
        lo = p * PAIR_W
        qm_ref[p, :, 0:LANES] = q[:, lo:lo + LANES].astype(BF16)
        qm_ref[p, :, LANES:PAIR_W] = _rope(q[:, lo + LANES:lo + PAIR_W], cos, sin).astype(BF16)
        km_ref[p, :, 0:LANES] = kv[:, p * LANES:(p + 1) * LANES].astype(BF16)
        km_ref[p, :, LANES:PAIR_W] = k_pe
        vm_ref[p] = kv[:, MLA_WIDTH + p * LANES:MLA_WIDTH + (p + 1) * LANES].astype(BF16)


def _in_proj_call(x, mod, g, w_in, gq, wq, gkv, wkv, cos_t, sin_t, seq, layer):
    t = x.shape[0]
    tm = ROW_TILE
    tiles_per_seq = seq // tm
    row = lambda i: (i, 0)
    pair_widths = (LANES, LANES, LANES, PAIR_W, PAIR_W, LANES)
    pair_shapes = tuple(jax.ShapeDtypeStruct((N_PAIRS, t, w), BF16) for w in pair_widths)
    pair_specs = tuple(pl.BlockSpec((N_PAIRS, tm, w), lambda i: (0, i, 0)) for w in pair_widths)
    gate_shape = jax.ShapeDtypeStruct((t, D_MODEL), BF16)
    gate_spec = pl.BlockSpec((tm, D_MODEL), row)
    return pl.pallas_call(
        _in_proj_kernel,
        out_shape=pair_shapes + (gate_shape, gate_shape),
        grid=(t // tm,),
        in_specs=[
            pl.BlockSpec((tm, D_MODEL), row),
            pl.BlockSpec((1, 1, N_MOD * D_MODEL), lambda i: (i // tiles_per_seq, 0, 0)),
            _resident((1, D_MODEL)),
            _resident((D_MODEL, IN_W), layer),
            _resident((1, MLA_Q_RANK)),
            _resident((MLA_Q_RANK, MLA_QK_W), layer),
            _resident((1, MLA_KV_RANK)),
            _resident((MLA_KV_RANK, 2 * MLA_WIDTH), layer),
            pl.BlockSpec((tm, LANES), row),
            pl.BlockSpec((tm, LANES), row),
        ],
        out_specs=pair_specs + (gate_spec, gate_spec),
        compiler_params=pltpu.CompilerParams(
            dimension_semantics=("parallel",), vmem_limit_bytes=VMEM_LIMIT),
        name="in_proj",
    )(x, mod, g, w_in, gq, wq, gkv, wkv, cos_t, sin_t)


def _causal_blocks(nq, depth):
    blocks = [(i, i) for i in range(nq)]
    blocks += [(i, j) for i in range(1, nq) for j in range(i - 1, -1, -1)]
    pad = depth - 1
    table = [blocks[0]] * pad + blocks + [blocks[-1]] * pad
    return (jnp.asarray([e[0] for e in table], jnp.int32),
            jnp.asarray([e[1] for e in table], jnp.int32), len(blocks))


def _block_offset(i):
    return pl.multiple_of(i * ATTN_TILE, ATTN_TILE)


def _run_pipeline(stages, it_ref, jt_ref, nq, n_blocks):
    depth = len(stages)

    def step(diag):
        def body(t, carry):
            for p in range(N_PAIRS):
                for k in range(depth - 1, -1, -1):
                    e = t - k + depth - 1
                    stages[k](p, it_ref[e], jt_ref[e], diag)
            return carry
        return body

    n_steps = n_blocks + depth - 1
    assert nq % LOOP_UNROLL == 0
    looped = nq + (n_steps - nq) // LOOP_UNROLL * LOOP_UNROLL
    lax.fori_loop(0, nq, step(True), 0, unroll=LOOP_UNROLL)
    lax.fori_loop(nq, looped, step(False), 0, unroll=LOOP_UNROLL)
    for t in range(looped, n_steps):
        step(False)(t, 0)


def _sb_attn_kernel(it_ref, jt_ref, q_ref, k_ref, v_ref, o_ref, later_ref, *scratch,
                    nq, n_blocks):
    blk = ATTN_TILE
    per_pair = [scratch[7 * p:7 * p + 7] for p in range(N_PAIRS)]
    acc_refs, r_refs, ls_refs, lf_refs, rs0_refs, rs1_refs, w_refs = zip(*per_pair)
    lane = lax.broadcasted_iota(jnp.int32, (1, LANES), 1)
    first = lane < HEAD_DIM
    row = lax.broadcasted_iota(jnp.int32, (blk, blk), 0)
    col = lax.broadcasted_iota(jnp.int32, (blk, blk), 1)
    row2 = lax.broadcasted_iota(jnp.int32, (2 * blk, blk), 0) & (blk - 1)
    col2 = lax.broadcasted_iota(jnp.int32, (2 * blk, blk), 1)
    strict = col2 < row2
    later_ref[...] = jnp.where(row > col, 1.0, 0.0).astype(BF16)
    for p in range(N_PAIRS):
        acc_refs[p][...] = jnp.zeros_like(acc_refs[p])
        r_refs[p][...] = jnp.zeros_like(r_refs[p])
        ls_refs[p][...] = jnp.full_like(ls_refs[p], MASKED)
        lf_refs[p][...] = jnp.zeros_like(lf_refs[p])
        rs0_refs[p][...] = jnp.zeros_like(rs0_refs[p])
        rs1_refs[p][...] = jnp.zeros_like(rs1_refs[p])
        w_refs[p][...] = jnp.zeros_like(w_refs[p])

    def scores(p, i, j, diag):
        q = q_ref[p, pl.ds(_block_offset(i), blk), :]
        zero = jnp.zeros_like(q)
        q2 = jnp.concatenate([jnp.where(first, q, zero), jnp.where(first, zero, q)], axis=0)
        z = _dot_nt(q2, k_ref[p, pl.ds(_block_offset(j), blk), :])
        neg_part = jnp.minimum(z, 0.0)
        neg_relu = neg_part - z
        softplus = jnp.log2(1.0 + jnp.exp2(neg_part + neg_relu))
        ls = neg_part - softplus
        lf = neg_relu - softplus
        if diag:
            ls = jnp.where(strict, ls, MASKED)
            lf = jnp.where(strict, lf, 0.0)
        rs = jnp.sum(lf, axis=1, keepdims=True)
        ls_refs[p][...] = ls
        lf_refs[p][...] = lf.astype(BF16)
        rs0_refs[p][...] = jnp.where(first, rs[0:blk], rs[blk:2 * blk])

    def weights(p, i, j, diag):
        cum = _dot(lf_refs[p][...], later_ref[...])
        w_refs[p][...] = jnp.exp2(ls_refs[p][...] + cum).astype(BF16)
        rs1_refs[p][...] = rs0_refs[p][...]

    def accumulate(p, i, j, diag):
        pv = _dot(w_refs[p][...], v_ref[p, pl.ds(_block_offset(j), blk), :])
        pv = jnp.where(first, pv[0:blk], pv[blk:2 * blk])
        r = r_refs[p][i]
        acc_refs[p][i] = acc_refs[p][i] + jnp.exp2(r) * pv
        r_refs[p][i] = r + rs1_refs[p][...]

    _run_pipeline((scores, weights, accumulate), it_ref, jt_ref, nq, n_blocks)
    for p in range(N_PAIRS):
        for i in range(nq):
            o_ref[p, i * blk:(i + 1) * blk, :] = acc_refs[p][i].astype(o_ref.dtype)


def _sb_attn_call(q, k, v):
    n_pairs, b, s, _ = q.shape
    blk = ATTN_TILE
    nq = s // blk
    it, jt, n_blocks = _causal_blocks(nq, 3)
    spec = pl.BlockSpec((n_pairs, None, s, LANES), lambda bi, *_: (0, bi, 0, 0))
    grid_spec = pltpu.PrefetchScalarGridSpec(
        num_scalar_prefetch=2,
        grid=(b,),
        in_specs=[spec, spec, spec],
        out_specs=spec,
        scratch_shapes=[pltpu.VMEM((blk, blk), BF16)] + n_pairs * [
            pltpu.VMEM((nq, blk, LANES), F32),
            pltpu.VMEM((nq, blk, LANES), F32),
            pltpu.VMEM((2 * blk, blk), F32),
            pltpu.VMEM((2 * blk, blk), BF16),
            pltpu.VMEM((blk, LANES), F32),
            pltpu.VMEM((blk, LANES), F32),
            pltpu.VMEM((2 * blk, blk), BF16),
        ],
    )
    return pl.pallas_call(
        functools.partial(_sb_attn_kernel, nq=nq, n_blocks=n_blocks),
        out_shape=jax.ShapeDtypeStruct(q.shape, BF16),
        grid_spec=grid_spec,
        compiler_params=pltpu.CompilerParams(
            dimension_semantics=("parallel",), vmem_limit_bytes=VMEM_LIMIT),
        name="sb_attn",
    )(it, jt, q, k, v)


def _mla_attn_kernel(it_ref, jt_ref, q_ref, k_ref, v_ref, o_ref, vblk_ref, *scratch,
                     nq, n_blocks):
    blk = ATTN_TILE
    per_pair = [scratch[4 * p:4 * p + 4] for p in range(N_PAIRS)]
    acc_refs, m_refs, alpha_refs, p_refs = zip(*per_pair)
    lane = lax.broadcasted_iota(jnp.int32, (1, LANES), 1)
    first = lane < HEAD_DIM
    lane2 = lax.broadcasted_iota(jnp.int32, (1, PAIR_W), 1)
    own_a = (lane2 < HEAD_DIM) | ((lane2 >= LANES) & (lane2 < LANES + MLA_ROPE_DIM))
    own_b = ((lane2 >= HEAD_DIM) & (lane2 < LANES)) | (
        (lane2 >= LANES + MLA_ROPE_DIM) & (lane2 < LANES + 2 * MLA_ROPE_DIM))
    row2 = lax.broadcasted_iota(jnp.int32, (2 * blk, blk), 0) & (blk - 1)
    col2 = lax.broadcasted_iota(jnp.int32, (2 * blk, blk), 1)
    causal = col2 <= row2
    sum_a = jnp.broadcast_to(jnp.where(first, 1.0, 0.0).astype(BF16), (blk, LANES))
    sum_b = jnp.broadcast_to(jnp.where(first, 0.0, 1.0).astype(BF16), (blk, LANES))
    for p in range(N_PAIRS):
        for j in range(nq):
            v = v_ref[p, j * blk:(j + 1) * blk, :]
            zero = jnp.zeros_like(v)
            lo = 2 * j * blk
            vblk_ref[p, lo:lo + blk, 0:LANES] = jnp.where(first, v, zero)
            vblk_ref[p, lo:lo + blk, LANES:PAIR_W] = sum_a
            vblk_ref[p, lo + blk:lo + 2 * blk, 0:LANES] = jnp.where(first, zero, v)
            vblk_ref[p, lo + blk:lo + 2 * blk, LANES:PAIR_W] = sum_b
        acc_refs[p][...] = jnp.zeros_like(acc_refs[p])
        m_refs[p][...] = jnp.full_like(m_refs[p], MASKED)
        alpha_refs[p][...] = jnp.zeros_like(alpha_refs[p])
        p_refs[p][...] = jnp.zeros_like(p_refs[p])

    def probabilities(p, i, j, diag):
        q = q_ref[p, pl.ds(_block_offset(i), blk), :]
        zero = jnp.zeros_like(q)
        q2 = jnp.concatenate([jnp.where(own_a, q, zero), jnp.where(own_b, q, zero)], axis=0)
        sc = _dot_nt(q2, k_ref[p, pl.ds(_block_offset(j), blk), :])
        if diag:
            sc = jnp.where(causal, sc, MASKED)
        m_old = m_refs[p][i]
        m_new = jnp.maximum(m_old, jnp.max(sc, axis=1, keepdims=True))
        alpha_refs[p][...] = jnp.exp2(m_old - m_new)
        m_refs[p][i] = m_new
        prob = jnp.exp2(sc - jnp.concatenate([m_new, m_new], axis=1)).astype(BF16)
        p_refs[p][:, 0:blk] = prob[0:blk]
        p_refs[p][:, blk:2 * blk] = prob[blk:2 * blk]

    def accumulate(p, i, j, diag):
        pv = _dot(p_refs[p][...], vblk_ref[p, pl.ds(pl.multiple_of(2 * j * blk, 2 * blk), 2 * blk), :])
        alpha = alpha_refs[p][...]
        alpha = jnp.where(first, alpha[0:blk], alpha[blk:2 * blk])
        acc_refs[p][i] = acc_refs[p][i] * jnp.concatenate([alpha, alpha], axis=1) + pv

    _run_pipeline((probabilities, accumulate), it_ref, jt_ref, nq, n_blocks)
    for p in range(N_PAIRS):
        for i in range(nq):
            a = acc_refs[p][i]
            o_ref[p, i * blk:(i + 1) * blk, :] = (a[:, 0:LANES] / a[:, LANES:PAIR_W]).astype(o_ref.dtype)


def _mla_attn_call(q, k, v):
    n_pairs, b, s, _ = v.shape
    blk = ATTN_TILE
    nq = s // blk
    it, jt, n_blocks = _causal_blocks(nq, 2)
    qk_spec = pl.BlockSpec((n_pairs, None, s, PAIR_W), lambda bi, *_: (0, bi, 0, 0))
    v_spec = pl.BlockSpec((n_pairs, None, s, LANES), lambda bi, *_: (0, bi, 0, 0))
    grid_spec = pltpu.PrefetchScalarGridSpec(
        num_scalar_prefetch=2,
        grid=(b,),
        in_specs=[qk_spec, qk_spec, v_spec],
        out_specs=v_spec,
        scratch_shapes=[pltpu.VMEM((n_pairs, 2 * s, PAIR_W), BF16)] + n_pairs * [
            pltpu.VMEM((nq, blk, PAIR_W), F32),
            pltpu.VMEM((nq, 2 * blk, LANES), F32),
            pltpu.VMEM((2 * blk, LANES), F32),
            pltpu.VMEM((blk, 2 * blk), BF16),
        ],
    )
    return pl.pallas_call(
        functools.partial(_mla_attn_kernel, nq=nq, n_blocks=n_blocks),
        out_shape=jax.ShapeDtypeStruct(v.shape, BF16),
        grid_spec=grid_spec,
        compiler_params=pltpu.CompilerParams(
            dimension_semantics=("parallel",), vmem_limit_bytes=VMEM_LIMIT),
        name="mla_attn",
    )(it, jt, q, k, v)


def _post_kernel(x_ref, osb_ref, omla_ref, sgs_ref, sgm_ref, mod_ref,
                 wsb_ref, wmla_ref, wmix_ref, g2_ref, wup_ref, wdn_ref, gf_ref,
                 o_ref, *, final):
    mod = mod_ref[0]
    gate1 = mod[:, 2 * D_MODEL:3 * D_MODEL]
    shift2 = mod[:, 3 * D_MODEL:4 * D_MODEL]
    scale2 = mod[:, 4 * D_MODEL:5 * D_MODEL]
    gate2 = mod[:, 5 * D_MODEL:6 * D_MODEL]

    o_sb = _dot(jnp.concatenate([osb_ref[p] for p in range(N_PAIRS)], axis=1), wsb_ref[...])
    o_mla = _dot(jnp.concatenate([omla_ref[p] for p in range(N_PAIRS)], axis=1), wmla_ref[...])
    merged = sgs_ref[...].astype(F32) * o_sb + sgm_ref[...].astype(F32) * o_mla
    x = x_ref[...] + gate1 * _dot(merged.astype(BF16), wmix_ref[...])

    h = _rms(x, g2_ref[...]) * (1.0 + scale2) + shift2
    u = jnp.maximum(_dot(h.astype(BF16), wup_ref[...]), 0.0)
    x = x + gate2 * _dot((u * u).astype(BF16), wdn_ref[...])
    if final:
        x = _rms(x, gf_ref[...])
    o_ref[...] = x


def _post_call(x, o_sb, o_mla, sg_sb, sg_mla, mod, w_sb, w_mla, w_mix, g2, w_up, w_dn, g_final,
               seq, layer, final):
    t = x.shape[0]
    tm = ROW_TILE
    tiles_per_seq = seq // tm
    row = lambda i: (i, 0)
    pair_spec = pl.BlockSpec((N_PAIRS, tm, LANES), lambda i: (0, i, 0))
    return pl.pallas_call(
        functools.partial(_post_kernel, final=final),
        out_shape=jax.ShapeDtypeStruct((t, D_MODEL), F32),
        grid=(t // tm,),
        in_specs=[
            pl.BlockSpec((tm, D_MODEL), row),
            pair_spec,
            pair_spec,
            pl.BlockSpec((tm, D_MODEL), row),
            pl.BlockSpec((tm, D_MODEL), row),
            pl.BlockSpec((1, 1, N_MOD * D_MODEL), lambda i: (i // tiles_per_seq, 0, 0)),
            _resident((SB_WIDTH, D_MODEL), layer),
            _resident((MLA_WIDTH, D_MODEL), layer),
            _resident((D_MODEL, D_MODEL), layer),
            _resident((1, D_MODEL)),
            _resident((D_MODEL, D_FF), layer),
            _resident((D_FF, D_MODEL), layer),
            _resident((1, D_MODEL)),
        ],
        out_specs=pl.BlockSpec((tm, D_MODEL), row),
        compiler_params=pltpu.CompilerParams(
            dimension_semantics=("parallel",), vmem_limit_bytes=VMEM_LIMIT),
        name="post_final" if final else "post",
    )(x, o_sb, o_mla, sg_sb, sg_mla, mod, w_sb, w_mla, w_mix, g2, w_up, w_dn, g_final)


def _layout_w_in(w_in):
    depth, d, _ = w_in.shape
    head = w_in[..., 0:OFF_KROPE]
    k_rope = w_in[..., OFF_KROPE:OFF_KROPE + MLA_ROPE_DIM]
    pad = jnp.zeros((depth, d, LANES - 2 * MLA_ROPE_DIM), w_in.dtype)
    gates = w_in[..., OFF_KROPE + MLA_ROPE_DIM:]
    return jnp.concatenate([head, k_rope, k_rope, pad, gates], axis=-1).astype(BF16)


def _layout_w_q_up(w_q_up):
    depth, r, _ = w_q_up.shape
    w = w_q_up.reshape(depth, r, N_PAIRS, 2, MLA_QK_DIM)
    nope = w[..., :HEAD_DIM].reshape(depth, r, N_PAIRS, LANES)
    rope = w[..., HEAD_DIM:].reshape(depth, r, N_PAIRS, 2 * MLA_ROPE_DIM)
    pad = jnp.zeros((depth, r, N_PAIRS, LANES - 2 * MLA_ROPE_DIM), w.dtype)
    return jnp.concatenate([nope, rope, pad], axis=-1).reshape(depth, r, MLA_QK_W).astype(BF16)


def _layout_w_kv_up(w_kv_up):
    depth, r, _ = w_kv_up.shape
    w = w_kv_up.reshape(depth, r, MLA_HEADS, 2 * HEAD_DIM)
    k = w[..., :HEAD_DIM].reshape(depth, r, MLA_WIDTH)
    v = w[..., HEAD_DIM:].reshape(depth, r, MLA_WIDTH)
    return jnp.concatenate([k, v], axis=-1).astype(BF16)


def kernel(x, c, positions, w_ada, b_ada, g_mix_norm, w_in, g_q_lat, w_q_up, g_kv_lat, w_kv_up,
           w_sb_out, w_mla_out, w_mix_out, g_mlp_norm, w_up, w_down, g_final):
    batch, seq, d = x.shape
    depth = w_in.shape[0]
    t = batch * seq

    w_in_l = _layout_w_in(w_in)
    w_q_l = _layout_w_q_up(w_q_up)
    w_kv_l = _layout_w_kv_up(w_kv_up)
    w_sb_b, w_mla_b, w_mix_b = (w.astype(BF16) for w in (w_sb_out, w_mla_out, w_mix_out))
    w_up_b, w_dn_b = w_up.astype(BF16), w_down.astype(BF16)

    mod_all = _mod_call(c, w_ada, b_ada).reshape(depth, batch, 1, N_MOD * d)
    cos_t, sin_t = _rope_tables(positions)

    def per_batch(a):
        return a.reshape(N_PAIRS, batch, seq, a.shape[-1])

    xt = x.reshape(t, d)
    for l in range(depth):
        mod = mod_all[l]
        q_sb, k_sb, v_sb, q_m, k_m, v_m, sg_sb, sg_mla = _in_proj_call(
            xt, mod, g_mix_norm[l].reshape(1, d), w_in_l,
            g_q_lat[l].reshape(1, -1), w_q_l, g_kv_lat[l].reshape(1, -1), w_kv_l,
            cos_t, sin_t, seq, l)
        o_sb = _sb_attn_call(per_batch(q_sb), per_batch(k_sb), per_batch(v_sb))
        o_mla = _mla_attn_call(per_batch(q_m), per_batch(k_m), per_batch(v_m))
        xt = _post_call(xt, o_sb.reshape(N_PAIRS, t, LANES), o_mla.reshape(N_PAIRS, t, LANES),
                        sg_sb, sg_mla, mod,
                        w_sb_b, w_mla_b, w_mix_b, g_mlp_norm[l].reshape(1, d),
                        w_up_b, w_dn_b, g_final.reshape(1, d), seq, l, l == depth - 1)
    return xt.reshape(batch, seq, d)
```

```python
import functools
import math

import jax
import jax.numpy as jnp
from jax import lax
from jax.experimental import pallas as pl
from jax.experimental.pallas import tpu as pltpu

D_MODEL = 1024
SB_HEADS = 8
HEAD_DIM = 64
SB_WIDTH = SB_HEADS * HEAD_DIM
MLA_HEADS = 8
MLA_ROPE_DIM = 32
MLA_QK_DIM = HEAD_DIM + MLA_ROPE_DIM
MLA_Q_RANK = 384
MLA_KV_RANK = 256
MLA_WIDTH = MLA_HEADS * HEAD_DIM
D_FF = 4 * D_MODEL
ROPE_THETA = 10000.0
NORM_EPS = 1e-6
N_MOD = 6
LOG2E = math.log2(math.e)

LANES = 128
PAIR_W = 2 * LANES
N_PAIRS = SB_HEADS // 2
MLA_QK_W = N_PAIRS * PAIR_W

OFF_QKV = 0
OFF_QLAT = 3 * SB_WIDTH
OFF_KVLAT = OFF_QLAT + MLA_Q_RANK
OFF_KROPE = OFF_KVLAT + MLA_KV_RANK
OFF_GATES = OFF_KROPE + LANES
IN_W = OFF_GATES + 2 * D_MODEL

ROW_TILE = 512
ATTN_TILE = 256
LOOP_UNROLL = 4
VMEM_LIMIT = 56 * 1024 * 1024
MASKED = -1e30

F32 = jnp.float32
BF16 = jnp.bfloat16


def _resident(shape, layer=None):
    nd = len(shape)
    if layer is None:
        return pl.BlockSpec(shape, lambda *_: (0,) * nd, pipeline_mode=pl.Buffered(1))
    return pl.BlockSpec((None,) + tuple(shape), lambda *_: (layer,) + (0,) * nd,
                        pipeline_mode=pl.Buffered(1))


def _rms(x, g):
    return x * lax.rsqrt(jnp.mean(x * x, axis=-1, keepdims=True) + NORM_EPS) * g


def _dot(a, b):
    return jnp.dot(a, b, preferred_element_type=F32)


def _dot_nt(a, b):
    return lax.dot_general(a, b, (((1,), (1,)), ((), ())), preferred_element_type=F32)


def _mod_kernel(c_ref, w_ref, b_ref, o_ref):
    c = c_ref[...]
    c_act = c * (1.0 / (1.0 + jnp.exp(-c)))
    o_ref[0] = _dot(c_act.astype(BF16), w_ref[0].astype(BF16)) + b_ref[0]


def _mod_call(c, w_ada, b_ada):
    depth, d, n = w_ada.shape
    batch = c.shape[0]
    tn = 1536
    return pl.pallas_call(
        _mod_kernel,
        out_shape=jax.ShapeDtypeStruct((depth, batch, n), F32),
        grid=(depth, n // tn),
        in_specs=[
            pl.BlockSpec((batch, d), lambda l, j: (0, 0)),
            pl.BlockSpec((1, d, tn), lambda l, j: (l, 0, j)),
            pl.BlockSpec((1, 1, tn), lambda l, j: (l, 0, j)),
        ],
        out_specs=pl.BlockSpec((1, batch, tn), lambda l, j: (l, 0, j)),
        compiler_params=pltpu.CompilerParams(
            dimension_semantics=("parallel", "parallel"), vmem_limit_bytes=VMEM_LIMIT),
        name="adaln_mod",
    )(c, w_ada, b_ada.reshape(depth, 1, n))


def _rope_table_kernel(pos_ref, inv_ref, sign_ref, cos_ref, sin_ref):
    ang = pos_ref[...].astype(F32) * inv_ref[...]
    cos_ref[...] = jnp.cos(ang)
    sin_ref[...] = jnp.sin(ang) * sign_ref[...]


def _rope_tables(positions):
    t = positions.size
    half = MLA_ROPE_DIM // 2
    inv_freq = 1.0 / (ROPE_THETA ** (jnp.arange(0, MLA_ROPE_DIM, 2, dtype=F32) / MLA_ROPE_DIM))
    lane = jnp.arange(LANES)
    inv_lane = inv_freq[lane % half].reshape(1, LANES)
    sign_lane = jnp.where((lane % MLA_ROPE_DIM) < half, -1.0, 1.0).astype(F32).reshape(1, LANES)
    ts = 2048
    return pl.pallas_call(
        _rope_table_kernel,
        out_shape=(jax.ShapeDtypeStruct((t, LANES), F32),) * 2,
        grid=(t // ts,),
        in_specs=[
            pl.BlockSpec((ts, 1), lambda i: (i, 0)),
            pl.BlockSpec((1, LANES), lambda i: (0, 0)),
            pl.BlockSpec((1, LANES), lambda i: (0, 0)),
        ],
        out_specs=(pl.BlockSpec((ts, LANES), lambda i: (i, 0)),) * 2,
        compiler_params=pltpu.CompilerParams(dimension_semantics=("parallel",)),
        name="rope_tables",
    )(positions.reshape(t, 1), inv_lane, sign_lane)


def _rope(t, cos, sin_signed):
    lane = lax.broadcasted_iota(jnp.int32, (1, LANES), 1)
    low = (lane % MLA_ROPE_DIM) < (MLA_ROPE_DIM // 2)
    half = MLA_ROPE_DIM // 2
    swapped = jnp.where(low, pltpu.roll(t, LANES - half, 1), pltpu.roll(t, half, 1))
    return t * cos + swapped * sin_signed


def _in_proj_kernel(x_ref, mod_ref, g_ref, w_ref, gq_ref, wq_ref, gkv_ref, wkv_ref,
                    cos_ref, sin_ref,
                    qsb_ref, ksb_ref, vsb_ref, qm_ref, km_ref, vm_ref, sgs_ref, sgm_ref):
    mod = mod_ref[0]
    shift1 = mod[:, 0:D_MODEL]
    scale1 = mod[:, D_MODEL:2 * D_MODEL]
    h = _rms(x_ref[...], g_ref[...]) * (1.0 + scale1) + shift1
    hb = h.astype(BF16)

    lat = _dot(hb, w_ref[:, OFF_QLAT:OFF_GATES])
    q_lat = lat[:, 0:MLA_Q_RANK]
    kv_lat = lat[:, MLA_Q_RANK:MLA_Q_RANK + MLA_KV_RANK]
    k_rope = lat[:, MLA_Q_RANK + MLA_KV_RANK:]

    qkv = _dot(hb, w_ref[:, OFF_QKV:OFF_QLAT])
    for p in range(N_PAIRS):
        lo = p * LANES
        qsb_ref[p] = (qkv[:, lo:lo + LANES] * (HEAD_DIM ** -0.5 * LOG2E)).astype(BF16)
        ksb_ref[p] = qkv[:, SB_WIDTH + lo:SB_WIDTH + lo + LANES].astype(BF16)
        vsb_ref[p] = qkv[:, 2 * SB_WIDTH + lo:2 * SB_WIDTH + lo + LANES].astype(BF16)

    gates = _dot(hb, w_ref[:, OFF_GATES:IN_W])
    sig = 1.0 / (1.0 + jnp.exp(-gates))
    sgs_ref[...] = sig[:, 0:D_MODEL].astype(BF16)
    sgm_ref[...] = sig[:, D_MODEL:2 * D_MODEL].astype(BF16)

    cos = cos_ref[...]
    sin = sin_ref[...]

    q = _dot(_rms(q_lat, gq_ref[...]).astype(BF16), wq_ref[...]) * (MLA_QK_DIM ** -0.5 * LOG2E)
    kv = _dot(_rms(kv_lat, gkv_ref[...]).astype(BF16), wkv_ref[...])
    k_pe = _rope(k_rope, cos, sin).astype(BF16)
    for p in range(N_PAIRS):
        lo = p * PAIR_W
        qm_ref[p, :, 0:LANES] = q[:, lo:lo + LANES].astype(BF16)
        qm_ref[p, :, LANES:PAIR_W] = _rope(q[:, lo + LANES:lo + PAIR_W], cos, sin).astype(BF16)
        km_ref[p, :, 0:LANES] = kv[:, p * LANES:(p + 1) * LANES].astype(BF16)
        km_ref[p, :, LANES:PAIR_W] = k_pe
        vm_ref[p] = kv[:, MLA_WIDTH + p * LANES:MLA_WIDTH + (p + 1) * LANES].astype(BF16)


def _in_proj_call(x, mod, g, w_in, gq, wq, gkv, wkv, cos_t, sin_t, seq, layer):
    t = x.shape[0]
    tm = ROW_TILE
    tiles_per_seq = seq // tm
    row = lambda i: (i, 0)
    pair_widths = (LANES, LANES, LANES, PAIR_W, PAIR_W, LANES)
    pair_shapes = tuple(jax.ShapeDtypeStruct((N_PAIRS, t, w), BF16) for w in pair_widths)
    pair_specs = tuple(pl.BlockSpec((N_PAIRS, tm, w), lambda i: (0, i, 0)) for w in pair_widths)
    gate_shape = jax.ShapeDtypeStruct((t, D_MODEL), BF16)
    gate_spec = pl.BlockSpec((tm, D_MODEL), row)
    return pl.pallas_call(
        _in_proj_kernel,
        out_shape=pair_shapes + (gate_shape, gate_shape),
        grid=(t // tm,),
        in_specs=[
            pl.BlockSpec((tm, D_MODEL), row),
            pl.BlockSpec((1, 1, N_MOD * D_MODEL), lambda i: (i // tiles_per_seq, 0, 0)),
            _resident((1, D_MODEL)),
            _resident((D_MODEL, IN_W), layer),
            _resident((1, MLA_Q_RANK)),
            _resident((MLA_Q_RANK, MLA_QK_W), layer),
            _resident((1, MLA_KV_RANK)),
            _resident((MLA_KV_RANK, 2 * MLA_WIDTH), layer),
            pl.BlockSpec((tm, LANES), row),
            pl.BlockSpec((tm, LANES), row),
        ],
        out_specs=pair_specs + (gate_spec, gate_spec),
        compiler_params=pltpu.CompilerParams(
            dimension_semantics=("parallel",), vmem_limit_bytes=VMEM_LIMIT),
        name="in_proj",
    )(x, mod, g, w_in, gq, wq, gkv, wkv, cos_t, sin_t)


def _causal_blocks(nq, depth):
    blocks = [(i, i) for i in range(nq)]
    blocks += [(i, j) for i in range(1, nq) for j in range(i - 1, -1, -1)]
    pad = depth - 1
    table = [blocks[0]] * pad + blocks + [blocks[-1]] * pad
    return (jnp.asarray([e[0] for e in table], jnp.int32),
            jnp.asarray([e[1] for e in table], jnp.int32), len(blocks))


def _block_offset(i):
    return pl.multiple_of(i * ATTN_TILE, ATTN_TILE)


def _run_pipeline(stages, it_ref, jt_ref, nq, n_blocks, pair_major):
    depth = len(stages)

    def pair_step(p, t, diag):
        for k in range(depth - 1, -1, -1):
            e = t - k + depth - 1
            stages[k](p, it_ref[e], jt_ref[e], diag)

    def steps(t0, n, diag):
        if pair_major:
            order = [(p, dt) for p in range(N_PAIRS) for dt in range(n)]
        else:
            order = [(p, dt) for dt in range(n) for p in range(N_PAIRS)]
        for p, dt in order:
            pair_step(p, t0 + dt, diag)

    def chunk(base, diag):
        def body(c, carry):
            steps(base + c * LOOP_UNROLL, LOOP_UNROLL, diag)
            return carry
        return body

    n_steps = n_blocks + depth - 1
    assert nq % LOOP_UNROLL == 0
    n_chunks = (n_steps - nq) // LOOP_UNROLL
    lax.fori_loop(0, nq // LOOP_UNROLL, chunk(0, True), 0)
    lax.fori_loop(0, n_chunks, chunk(nq, False), 0)
    looped = nq + n_chunks * LOOP_UNROLL
    steps(looped, n_steps - looped, False)


def _sb_attn_kernel(it_ref, jt_ref, q_ref, k_ref, v_ref, o_ref, later_ref, *scratch,
                    nq, n_blocks):
    blk = ATTN_TILE
    per_pair = [scratch[7 * p:7 * p + 7] for p in range(N_PAIRS)]
    acc_refs, r_refs, ls_refs, lf_refs, rs0_refs, rs1_refs, w_refs = zip(*per_pair)
    lane = lax.broadcasted_iota(jnp.int32, (1, LANES), 1)
    first = lane < HEAD_DIM
    row = lax.broadcasted_iota(jnp.int32, (blk, blk), 0)
    col = lax.broadcasted_iota(jnp.int32, (blk, blk), 1)
    row2 = lax.broadcasted_iota(jnp.int32, (2 * blk, blk), 0) & (blk - 1)
    col2 = lax.broadcasted_iota(jnp.int32, (2 * blk, blk), 1)
    strict = col2 < row2
    later_ref[...] = jnp.where(row > col, 1.0, 0.0).astype(BF16)
    for p in range(N_PAIRS):
        acc_refs[p][...] = jnp.zeros_like(acc_refs[p])
        r_refs[p][...] = jnp.zeros_like(r_refs[p])
        ls_refs[p][...] = jnp.full_like(ls_refs[p], MASKED)
        lf_refs[p][...] = jnp.zeros_like(lf_refs[p])
        rs0_refs[p][...] = jnp.zeros_like(rs0_refs[p])
        rs1_refs[p][...] = jnp.zeros_like(rs1_refs[p])
        w_refs[p][...] = jnp.zeros_like(w_refs[p])

    def scores(p, i, j, diag):
        q = q_ref[p, pl.ds(_block_offset(i), blk), :]
        zero = jnp.zeros_like(q)
        q2 = jnp.concatenate([jnp.where(first, q, zero), jnp.where(first, zero, q)], axis=0)
        z = _dot_nt(q2, k_ref[p, pl.ds(_block_offset(j), blk), :])
        neg_part = jnp.minimum(z, 0.0)
        neg_relu = neg_part - z
        softplus = jnp.log2(1.0 + jnp.exp2(neg_part + neg_relu))
        ls = neg_part - softplus
        lf = neg_relu - softplus
        if diag:
            ls = jnp.where(strict, ls, MASKED)
            lf = jnp.where(strict, lf, 0.0)
        rs = jnp.sum(lf, axis=1, keepdims=True)
        ls_refs[p][...] = ls
        lf_refs[p][...] = lf.astype(BF16)
        rs0_refs[p][...] = jnp.where(first, rs[0:blk], rs[blk:2 * blk])

    def weights(p, i, j, diag):
        cum = _dot(lf_refs[p][...], later_ref[...])
        w_refs[p][...] = jnp.exp2(ls_refs[p][...] + cum).astype(BF16)
        rs1_refs[p][...] = rs0_refs[p][...]

    def accumulate(p, i, j, diag):
        pv = _dot(w_refs[p][...], v_ref[p, pl.ds(_block_offset(j), blk), :])
        pv = jnp.where(first, pv[0:blk], pv[blk:2 * blk])
        r = r_refs[p][i]
        acc_refs[p][i] = acc_refs[p][i] + jnp.exp2(r) * pv
        r_refs[p][i] = r + rs1_refs[p][...]

    _run_pipeline((scores, weights, accumulate), it_ref, jt_ref, nq, n_blocks, pair_major=True)
    for p in range(N_PAIRS):
        for i in range(nq):
            o_ref[p, i * blk:(i + 1) * blk, :] = acc_refs[p][i].astype(o_ref.dtype)


def _sb_attn_call(q, k, v):
    n_pairs, b, s, _ = q.shape
    blk = ATTN_TILE
    nq = s // blk
    it, jt, n_blocks = _causal_blocks(nq, 3)
    spec = pl.BlockSpec((n_pairs, None, s, LANES), lambda bi, *_: (0, bi, 0, 0))
    grid_spec = pltpu.PrefetchScalarGridSpec(
        num_scalar_prefetch=2,
        grid=(b,),
        in_specs=[spec, spec, spec],
        out_specs=spec,
        scratch_shapes=[pltpu.VMEM((blk, blk), BF16)] + n_pairs * [
            pltpu.VMEM((nq, blk, LANES), F32),
            pltpu.VMEM((nq, blk, LANES), F32),
            pltpu.VMEM((2 * blk, blk), F32),
            pltpu.VMEM((2 * blk, blk), BF16),
            pltpu.VMEM((blk, LANES), F32),
            pltpu.VMEM((blk, LANES), F32),
            pltpu.VMEM((2 * blk, blk), BF16),
        ],
    )
    return pl.pallas_call(
        functools.partial(_sb_attn_kernel, nq=nq, n_blocks=n_blocks),
        out_shape=jax.ShapeDtypeStruct(q.shape, BF16),
        grid_spec=grid_spec,
        compiler_params=pltpu.CompilerParams(
            dimension_semantics=("parallel",), vmem_limit_bytes=VMEM_LIMIT),
        name="sb_attn",
    )(it, jt, q, k, v)


def _mla_attn_kernel(it_ref, jt_ref, q_ref, k_ref, v_ref, o_ref, vblk_ref, *scratch,
                     nq, n_blocks):
    blk = ATTN_TILE
    per_pair = [scratch[4 * p:4 * p + 4] for p in range(N_PAIRS)]
    acc_refs, m_refs, alpha_refs, p_refs = zip(*per_pair)
    lane = lax.broadcasted_iota(jnp.int32, (1, LANES), 1)
    first = lane < HEAD_DIM
    lane2 = lax.broadcasted_iota(jnp.int32, (1, PAIR_W), 1)
    own_a = (lane2 < HEAD_DIM) | ((lane2 >= LANES) & (lane2 < LANES + MLA_ROPE_DIM))
    own_b = ((lane2 >= HEAD_DIM) & (lane2 < LANES)) | (
        (lane2 >= LANES + MLA_ROPE_DIM) & (lane2 < LANES + 2 * MLA_ROPE_DIM))
    row2 = lax.broadcasted_iota(jnp.int32, (2 * blk, blk), 0) & (blk - 1)
    col2 = lax.broadcasted_iota(jnp.int32, (2 * blk, blk), 1)
    causal = col2 <= row2
    sum_a = jnp.broadcast_to(jnp.where(first, 1.0, 0.0).astype(BF16), (blk, LANES))
    sum_b = jnp.broadcast_to(jnp.where(first, 0.0, 1.0).astype(BF16), (blk, LANES))
    for p in range(N_PAIRS):
        for j in range(nq):
            v = v_ref[p, j * blk:(j + 1) * blk, :]
            zero = jnp.zeros_like(v)
            lo = 2 * j * blk
            vblk_ref[p, lo:lo + blk, 0:LANES] = jnp.where(first, v, zero)
            vblk_ref[p, lo:lo + blk, LANES:PAIR_W] = sum_a
            vblk_ref[p, lo + blk:lo + 2 * blk, 0:LANES] = jnp.where(first, zero, v)
            vblk_ref[p, lo + blk:lo + 2 * blk, LANES:PAIR_W] = sum_b
        acc_refs[p][...] = jnp.zeros_like(acc_refs[p])
        m_refs[p][...] = jnp.full_like(m_refs[p], MASKED)
        alpha_refs[p][...] = jnp.zeros_like(alpha_refs[p])
        p_refs[p][...] = jnp.zeros_like(p_refs[p])

    def probabilities(p, i, j, diag):
        q = q_ref[p, pl.ds(_block_offset(i), blk), :]
        zero = jnp.zeros_like(q)
        q2 = jnp.concatenate([jnp.where(own_a, q, zero), jnp.where(own_b, q, zero)], axis=0)
        sc = _dot_nt(q2, k_ref[p, pl.ds(_block_offset(j), blk), :])
        if diag:
            sc = jnp.where(causal, sc, MASKED)
        m_old = m_refs[p][i]
        m_new = jnp.maximum(m_old, jnp.max(sc, axis=1, keepdims=True))
        alpha_refs[p][...] = jnp.exp2(m_old - m_new)
        m_refs[p][i] = m_new
        prob = jnp.exp2(sc - jnp.concatenate([m_new, m_new], axis=1)).astype(BF16)
        p_refs[p][:, 0:blk] = prob[0:blk]
        p_refs[p][:, blk:2 * blk] = prob[blk:2 * blk]

    def accumulate(p, i, j, diag):
        pv = _dot(p_refs[p][...], vblk_ref[p, pl.ds(pl.multiple_of(2 * j * blk, 2 * blk), 2 * blk), :])
        alpha = alpha_refs[p][...]
        alpha = jnp.where(first, alpha[0:blk], alpha[blk:2 * blk])
        acc_refs[p][i] = acc_refs[p][i] * jnp.concatenate([alpha, alpha], axis=1) + pv

    _run_pipeline((probabilities, accumulate), it_ref, jt_ref, nq, n_blocks, pair_major=False)
    for p in range(N_PAIRS):
        for i in range(nq):
            a = acc_refs[p][i]
            o_ref[p, i * blk:(i + 1) * blk, :] = (a[:, 0:LANES] / a[:, LANES:PAIR_W]).astype(o_ref.dtype)


def _mla_attn_call(q, k, v):
    n_pairs, b, s, _ = v.shape
    blk = ATTN_TILE
    nq = s // blk
    it, jt, n_blocks = _causal_blocks(nq, 2)
    qk_spec = pl.BlockSpec((n_pairs, None, s, PAIR_W), lambda bi, *_: (0, bi, 0, 0))
    v_spec = pl.BlockSpec((n_pairs, None, s, LANES), lambda bi, *_: (0, bi, 0, 0))
    grid_spec = pltpu.PrefetchScalarGridSpec(
        num_scalar_prefetch=2,
        grid=(b,),
        in_specs=[qk_spec, qk_spec, v_spec],
        out_specs=v_spec,
        scratch_shapes=[pltpu.VMEM((n_pairs, 2 * s, PAIR_W), BF16)] + n_pairs * [
            pltpu.VMEM((nq, blk, PAIR_W), F32),
            pltpu.VMEM((nq, 2 * blk, LANES), F32),
            pltpu.VMEM((2 * blk, LANES), F32),
            pltpu.VMEM((blk, 2 * blk), BF16),
        ],
    )
    return pl.pallas_call(
        functools.partial(_mla_attn_kernel, nq=nq, n_blocks=n_blocks),
        out_shape=jax.ShapeDtypeStruct(v.shape, BF16),
        grid_spec=grid_spec,
        compiler_params=pltpu.CompilerParams(
            dimension_semantics=("parallel",), vmem_limit_bytes=VMEM_LIMIT),
        name="mla_attn",
    )(it, jt, q, k, v)


def _post_kernel(x_ref, osb_ref, omla_ref, sgs_ref, sgm_ref, mod_ref,
                 wsb_ref, wmla_ref, wmix_ref, g2_ref, wup_ref, wdn_ref, gf_ref,
                 o_ref, *, final):
    mod = mod_ref[0]
    gate1 = mod[:, 2 * D_MODEL:3 * D_MODEL]
    shift2 = mod[:, 3 * D_MODEL:4 * D_MODEL]
    scale2 = mod[:, 4 * D_MODEL:5 * D_MODEL]
    gate2 = mod[:, 5 * D_MODEL:6 * D_MODEL]

    o_sb = _dot(jnp.concatenate([osb_ref[p] for p in range(N_PAIRS)], axis=1), wsb_ref[...])
    o_mla = _dot(jnp.concatenate([omla_ref[p] for p in range(N_PAIRS)], axis=1), wmla_ref[...])
    merged = sgs_ref[...].astype(F32) * o_sb + sgm_ref[...].astype(F32) * o_mla
    x = x_ref[...] + gate1 * _dot(merged.astype(BF16), wmix_ref[...])

    h = _rms(x, g2_ref[...]) * (1.0 + scale2) + shift2
    u = jnp.maximum(_dot(h.astype(BF16), wup_ref[...]), 0.0)
    x = x + gate2 * _dot((u * u).astype(BF16), wdn_ref[...])
    if final:
        x = _rms(x, gf_ref[...])
    o_ref[...] = x


def _post_call(x, o_sb, o_mla, sg_sb, sg_mla, mod, w_sb, w_mla, w_mix, g2, w_up, w_dn, g_final,
               seq, layer, final):
    t = x.shape[0]
    tm = ROW_TILE
    tiles_per_seq = seq // tm
    row = lambda i: (i, 0)
    pair_spec = pl.BlockSpec((N_PAIRS, tm, LANES), lambda i: (0, i, 0))
    return pl.pallas_call(
        functools.partial(_post_kernel, final=final),
        out_shape=jax.ShapeDtypeStruct((t, D_MODEL), F32),
        grid=(t // tm,),
        in_specs=[
            pl.BlockSpec((tm, D_MODEL), row),
            pair_spec,
            pair_spec,
            pl.BlockSpec((tm, D_MODEL), row),
            pl.BlockSpec((tm, D_MODEL), row),
            pl.BlockSpec((1, 1, N_MOD * D_MODEL), lambda i: (i // tiles_per_seq, 0, 0)),
            _resident((SB_WIDTH, D_MODEL), layer),
            _resident((MLA_WIDTH, D_MODEL), layer),
            _resident((D_MODEL, D_MODEL), layer),
            _resident((1, D_MODEL)),
            _resident((D_MODEL, D_FF), layer),
            _resident((D_FF, D_MODEL), layer),
            _resident((1, D_MODEL)),
        ],
        out_specs=pl.BlockSpec((tm, D_MODEL), row),
        compiler_params=pltpu.CompilerParams(
            dimension_semantics=("parallel",), vmem_limit_bytes=VMEM_LIMIT),
        name="post_final" if final else "post",
    )(x, o_sb, o_mla, sg_sb, sg_mla, mod, w_sb, w_mla, w_mix, g2, w_up, w_dn, g_final)


def _layout_w_in(w_in):
    depth, d, _ = w_in.shape
    head = w_in[..., 0:OFF_KROPE]
    k_rope = w_in[..., OFF_KROPE:OFF_KROPE + MLA_ROPE_DIM]
    pad = jnp.zeros((depth, d, LANES - 2 * MLA_ROPE_DIM), w_in.dtype)
    gates = w_in[..., OFF_KROPE + MLA_ROPE_DIM:]
    return jnp.concatenate([head, k_rope, k_rope, pad, gates], axis=-1).astype(BF16)


def _layout_w_q_up(w_q_up):
    depth, r, _ = w_q_up.shape
    w = w_q_up.reshape(depth, r, N_PAIRS, 2, MLA_QK_DIM)
    nope = w[..., :HEAD_DIM].reshape(depth, r, N_PAIRS, LANES)
    rope = w[..., HEAD_DIM:].reshape(depth, r, N_PAIRS, 2 * MLA_ROPE_DIM)
    pad = jnp.zeros((depth, r, N_PAIRS, LANES - 2 * MLA_ROPE_DIM), w.dtype)
    return jnp.concatenate([nope, rope, pad], axis=-1).reshape(depth, r, MLA_QK_W).astype(BF16)


def _layout_w_kv_up(w_kv_up):
    depth, r, _ = w_kv_up.shape
    w = w_kv_up.reshape(depth, r, MLA_HEADS, 2 * HEAD_DIM)
    k = w[..., :HEAD_DIM].reshape(depth, r, MLA_WIDTH)
    v = w[..., HEAD_DIM:].reshape(depth, r, MLA_WIDTH)
    return jnp.concatenate([k, v], axis=-1).astype(BF16)


def kernel(x, c, positions, w_ada, b_ada, g_mix_norm, w_in, g_q_lat, w_q_up, g_kv_lat, w_kv_up,
           w_sb_out, w_mla_out, w_mix_out, g_mlp_norm, w_up, w_down, g_final):
    batch, seq, d = x.shape
    depth = w_in.shape[0]
    t = batch * seq

    w_in_l = _layout_w_in(w_in)
    w_q_l = _layout_w_q_up(w_q_up)
    w_kv_l = _layout_w_kv_up(w_kv_up)
    w_sb_b, w_mla_b, w_mix_b = (w.astype(BF16) for w in (w_sb_out, w_mla_out, w_mix_out))
    w_up_b, w_dn_b = w_up.astype(BF16), w_down.astype(BF16)

    mod_all = _mod_call(c, w_ada, b_ada).reshape(depth, batch, 1, N_MOD * d)
    cos_t, sin_t = _rope_tables(positions)

    def per_batch(a):
        return a.reshape(N_PAIRS, batch, seq, a.shape[-1])

    xt = x.reshape(t, d)
    for l in range(depth):
        mod = mod_all[l]
        q_sb, k_sb, v_sb, q_m, k_m, v_m, sg_sb, sg_mla = _in_proj_call(
            xt, mod, g_mix_norm[l].reshape(1, d), w_in_l,
            g_q_lat[l].reshape(1, -1), w_q_l, g_kv_lat[l].reshape(1, -1), w_kv_l,
            cos_t, sin_t, seq, l)
        o_sb = _sb_attn_call(per_batch(q_sb), per_batch(k_sb), per_batch(v_sb))
        o_mla = _mla_attn_call(per_batch(q_m), per_batch(k_m), per_batch(v_m))
        xt = _post_call(xt, o_sb.reshape(N_PAIRS, t, LANES), o_mla.reshape(N_PAIRS, t, LANES),
                        sg_sb, sg_mla, mod,
                        w_sb_b, w_mla_b, w_mix_b, g_mlp_norm[l].reshape(1, d),
                        w_up_b, w_dn_b, g_final.reshape(1, d), seq, l, l == depth - 1)
    return xt.reshape(batch, seq, d)
```
